```python
import math
import jax, jax.numpy as jnp
from jax import lax
import numpy as np

D_MODEL = 1024
BATCH = 8
SEQ = 4096
DEPTH = 4

N_A = DEPTH // 2
N_B = DEPTH - N_A
FOX_HEADS = 16
FOX_HEAD_DIM = D_MODEL // FOX_HEADS
DIFF_HEADS = 8
DIFF_HEAD_DIM = D_MODEL // (2 * DIFF_HEADS)
DIFF_V_DIM = 2 * DIFF_HEAD_DIM
N_EXPERTS = 16
N_GROUPS = 4
EXPERTS_PER_GROUP = N_EXPERTS // N_GROUPS
TOP_K = 2
D_EXPERT = 512
Q_BLOCK = 128
DEEPNORM_ALPHA = (2 * DEPTH) ** 0.25
DEEPNORM_BETA = (8 * DEPTH) ** -0.25
LN_EPS = 1e-5
NEG_INF = -1e30

kernel_name = "yoco_fox_diffattn_grouped_moe_deepnorm_adaln"


def layer_norm(x, g, b):
    xf = x.astype(jnp.float32)
    mu = jnp.mean(xf, axis=-1, keepdims=True)
    var = jnp.mean(jnp.square(xf - mu), axis=-1, keepdims=True)
    y = (xf - mu) * lax.rsqrt(var + LN_EPS)
    return (y * g.astype(jnp.float32) + b.astype(jnp.float32)).astype(x.dtype)


def ada_params(c, w, b):
    return jax.nn.silu(c) @ w + b


def modulate(x, shift, scale):
    return x * (1.0 + scale[:, None, :]) + shift[:, None, :]


def post_norm_residual(x, y, gate, g, b):
    return layer_norm(DEEPNORM_ALPHA * x + (1.0 + gate[:, None, :]) * y, g, b)


def fox_attention(q, k, v, logf):
    B, H, S, Dh = q.shape
    nblk = S // Q_BLOCK
    f32 = jnp.float32
    cum = jnp.cumsum(logf.astype(f32), axis=-1)
    kf = k.astype(f32)
    vf = v.astype(f32)
    qb = (q.astype(f32) * (Dh ** -0.5)).reshape(B, H, nblk, Q_BLOCK, Dh).transpose(2, 0, 1, 3, 4)
    cb = cum.reshape(B, H, nblk, Q_BLOCK).transpose(2, 0, 1, 3)
    k_pos = jnp.arange(S)

    def one_block(args):
        i, q_i, c_i = args
        q_pos = i * Q_BLOCK + jnp.arange(Q_BLOCK)
        logits = jnp.einsum('bhqd,bhkd->bhqk', q_i, kf) + c_i[..., :, None] - cum[..., None, :]
        logits = jnp.where(k_pos[None, :] <= q_pos[:, None], logits, NEG_INF)
        p = jax.nn.softmax(logits, axis=-1)
        return jnp.einsum('bhqk,bhkd->bhqd', p, vf)

    out = lax.map(one_block, (jnp.arange(nblk), qb, cb))
    return out.transpose(1, 2, 0, 3, 4).reshape(B, H, S, Dh).astype(q.dtype)


def diff_attention(q1, q2, k1, k2, v, lam):
    B, H, S, d = q1.shape
    nblk = S // Q_BLOCK
    f32 = jnp.float32
    slopes = jnp.exp2(-8.0 * jnp.arange(1, H + 1, dtype=f32) / H)
    scale = d ** -0.5
    k1f, k2f, vf = k1.astype(f32), k2.astype(f32), v.astype(f32)
    q1b = (q1.astype(f32) * scale).reshape(B, H, nblk, Q_BLOCK, d).transpose(2, 0, 1, 3, 4)
    q2b = (q2.astype(f32) * scale).reshape(B, H, nblk, Q_BLOCK, d).transpose(2, 0, 1, 3, 4)
    k_pos = jnp.arange(S)

    def one_block(args):
        i, q1_i, q2_i = args
        q_pos = i * Q_BLOCK + jnp.arange(Q_BLOCK)
        dist = (q_pos[:, None] - k_pos[None, :]).astype(f32)
        bias = jnp.where(dist >= 0, -slopes[:, None, None] * dist, NEG_INF)
        p1 = jax.nn.softmax(jnp.einsum('bhqd,bhkd->bhqk', q1_i, k1f) + bias, axis=-1)
        p2 = jax.nn.softmax(jnp.einsum('bhqd,bhkd->bhqk', q2_i, k2f) + bias, axis=-1)
        return jnp.einsum('bhqk,bhkd->bhqd', p1 - lam * p2, vf)

    out = lax.map(one_block, (jnp.arange(nblk), q1b, q2b))
    return out.transpose(1, 2, 0, 3, 4).reshape(B, H, S, 2 * d)


def grouped_moe(h, router_w, router_b, w_gate, w_up, w_down):
    B, S, D = h.shape
    t = h.reshape(-1, D)
    T = t.shape[0]
    probs = jax.nn.softmax((t @ router_w).astype(jnp.float32), axis=-1)
    sel = probs + router_b.astype(jnp.float32)[None, :]
    grp = sel.reshape(T, N_GROUPS, EXPERTS_PER_GROUP)
    grp_score = jnp.sum(lax.top_k(grp, TOP_K)[0], axis=-1)
    g_idx = jnp.argmax(grp_score, axis=-1)
    in_grp = jnp.take_along_axis(grp, g_idx[:, None, None], axis=1)[:, 0]
    _, local = lax.top_k(in_grp, TOP_K)
    expert_idx = g_idx[:, None] * EXPERTS_PER_GROUP + local
    w = jnp.take_along_axis(probs, expert_idx, axis=-1)
    w = w / jnp.sum(w, axis=-1, keepdims=True)
    gates = jnp.sum(jax.nn.one_hot(expert_idx, N_EXPERTS, dtype=jnp.float32) * w[..., None], axis=1)
    gates = gates.astype(t.dtype)
    out = jnp.zeros_like(t)
    for e in range(N_EXPERTS):
        a = jax.nn.silu(t @ w_gate[e]) * (t @ w_up[e])
        out = out + gates[:, e:e + 1] * (a @ w_down[e])
    return out.reshape(B, S, D)


def setup_inputs(seed: int = 0) -> dict:
    key = jax.random.key(seed)
    keys = jax.random.split(key, 32)
    f32 = jnp.float32
    D = D_MODEL
    kd = DIFF_HEADS * DIFF_HEAD_DIM

    def normal(i, shape, scale):
        return scale * jax.random.normal(keys[i], shape, f32)

    x = normal(0, (BATCH, SEQ, D), 1.0)
    c = normal(1, (BATCH, D), 1.0)
    ada_w = normal(2, (DEPTH, D, 6 * D), 0.5 * D ** -0.5)
    ada_b = normal(3, (DEPTH, 6 * D), 0.02)
    ln_attn_g = 1.0 + normal(4, (DEPTH, D), 0.02)
    ln_attn_b = normal(5, (DEPTH, D), 0.02)
    ln_ffn_g = 1.0 + normal(6, (DEPTH, D), 0.02)
    ln_ffn_b = normal(7, (DEPTH, D), 0.02)
    fox_w_in = jnp.concatenate([
        normal(8, (N_A, D, 2 * D), D ** -0.5),
        normal(9, (N_A, D, D), DEEPNORM_BETA * D ** -0.5),
        normal(10, (N_A, D, FOX_HEADS), D ** -0.5),
    ], axis=-1)
    fox_b_f = 3.0 + normal(11, (N_A, FOX_HEADS), 1.0)
    fox_w_o = normal(12, (N_A, D, D), DEEPNORM_BETA * D ** -0.5)
    kv_ada_w = normal(13, (D, 2 * D), 0.5 * D ** -0.5)
    kv_ada_b = normal(14, (2 * D,), 0.02)
    kv_w = jnp.concatenate([
        normal(15, (D, 2 * kd), D ** -0.5),
        normal(16, (D, DIFF_HEADS * DIFF_V_DIM), DEEPNORM_BETA * D ** -0.5),
    ], axis=-1)
    diff_w_q = normal(17, (N_B, D, 2 * kd), D ** -0.5)
    diff_lambda = normal(18, (N_B, 4, DIFF_HEAD_DIM), 0.1)
    diff_norm_g = 1.0 + normal(19, (N_B, DIFF_V_DIM), 0.02)
    diff_w_o = normal(20, (N_B, DIFF_HEADS * DIFF_V_DIM, D), DEEPNORM_BETA * D ** -0.5)
    router_w = normal(21, (D, N_EXPERTS), D ** -0.5)
    router_b = normal(22, (N_EXPERTS,), 0.01)
    moe_w_gate = normal(23, (DEPTH, N_EXPERTS, D, D_EXPERT), D ** -0.5)
    moe_w_up = normal(24, (DEPTH, N_EXPERTS, D, D_EXPERT), D ** -0.5)
    moe_w_down = normal(25, (DEPTH, N_EXPERTS, D_EXPERT, D), DEEPNORM_BETA * D_EXPERT ** -0.5)
    return {"x": x, "c": c, "ada_w": ada_w, "ada_b": ada_b,
            "ln_attn_g": ln_attn_g, "ln_attn_b": ln_attn_b, "ln_ffn_g": ln_ffn_g, "ln_ffn_b": ln_ffn_b,
            "fox_w_in": fox_w_in, "fox_b_f": fox_b_f, "fox_w_o": fox_w_o,
            "kv_ada_w": kv_ada_w, "kv_ada_b": kv_ada_b, "kv_w": kv_w,
            "diff_w_q": diff_w_q, "diff_lambda": diff_lambda, "diff_norm_g": diff_norm_g, "diff_w_o": diff_w_o,
            "router_w": router_w, "router_b": router_b,
            "moe_w_gate": moe_w_gate, "moe_w_up": moe_w_up, "moe_w_down": moe_w_down}


def reference(x, c, ada_w, ada_b, ln_attn_g, ln_attn_b, ln_ffn_g, ln_ffn_b,
              fox_w_in, fox_b_f, fox_w_o, kv_ada_w, kv_ada_b, kv_w,
              diff_w_q, diff_lambda, diff_norm_g, diff_w_o,
              router_w, router_b, moe_w_gate, moe_w_up, moe_w_down):
    B, S, D = x.shape
    f32 = jnp.float32
    H_A, dh = FOX_HEADS, FOX_HEAD_DIM
    H_B, d = DIFF_HEADS, DIFF_HEAD_DIM
    kd = H_B * d

    def to_heads(t, h):
        return t.reshape(B, S, h, -1).transpose(0, 2, 1, 3)

    def from_heads(t):
        return t.transpose(0, 2, 1, 3).reshape(B, S, -1)

    k1 = k2 = v_sh = None
    for l in range(DEPTH):
        mod = ada_params(c, ada_w[l], ada_b[l])
        sh_a, sc_a, g_a, sh_f, sc_f, g_f = jnp.split(mod, 6, axis=-1)

        if l < N_A:
            i = l
            h = modulate(x, sh_a, sc_a)
            proj = h @ fox_w_in[i]
            q = to_heads(proj[..., :D], H_A)
            k = to_heads(proj[..., D:2 * D], H_A)
            v = to_heads(proj[..., 2 * D:3 * D], H_A)
            logf = jax.nn.log_sigmoid((proj[..., 3 * D:] + fox_b_f[i]).astype(f32))
            o = fox_attention(q, k, v, logf.transpose(0, 2, 1))
            y = from_heads(o) @ fox_w_o[i]
        else:
            if l == N_A:
                kv_sh, kv_sc = jnp.split(ada_params(c, kv_ada_w, kv_ada_b), 2, axis=-1)
                hk = modulate(x, kv_sh, kv_sc)
                kv = hk @ kv_w
                kk = kv[..., :2 * kd].reshape(B, S, H_B, 2, d)
                k1 = kk[..., 0, :].transpose(0, 2, 1, 3)
                k2 = kk[..., 1, :].transpose(0, 2, 1, 3)
                v_sh = to_heads(kv[..., 2 * kd:], H_B)
            j = l - N_A
            lam_init = 0.8 - 0.6 * math.exp(-0.3 * l)
            lp = diff_lambda[j].astype(f32)
            lam = jnp.exp(jnp.sum(lp[0] * lp[1])) - jnp.exp(jnp.sum(lp[2] * lp[3])) + lam_init
            h = modulate(x, sh_a, sc_a)
            qq = (h @ diff_w_q[j]).reshape(B, S, H_B, 2, d)
            q1 = qq[..., 0, :].transpose(0, 2, 1, 3)
            q2 = qq[..., 1, :].transpose(0, 2, 1, 3)
            o = diff_attention(q1, q2, k1, k2, v_sh, lam)
            o = o * lax.rsqrt(jnp.mean(jnp.square(o), axis=-1, keepdims=True) + LN_EPS)
            o = o * diff_norm_g[j].astype(f32) * (1.0 - lam_init)
            y = from_heads(o.astype(x.dtype)) @ diff_w_o[j]

        x = post_norm_residual(x, y, g_a, ln_attn_g[l], ln_attn_b[l])

        h = modulate(x, sh_f, sc_f)
        m = grouped_moe(h, router_w, router_b, moe_w_gate[l], moe_w_up[l], moe_w_down[l])
        x = post_norm_residual(x, m, g_f, ln_ffn_g[l], ln_ffn_b[l])
    return x
```

```python
import functools
import math

import jax
import jax.numpy as jnp
from jax import lax
from jax.experimental import pallas as pl
from jax.experimental.pallas import tpu as pltpu

F32 = jnp.float32
BF16 = jnp.bfloat16
HIGHEST = lax.Precision.HIGHEST

D_MODEL = 1024
DEPTH = 4
N_A = DEPTH // 2
HEAD_BLOCK = 128
N_HEAD_BLOCKS = D_MODEL // HEAD_BLOCK
HALF = HEAD_BLOCK // 2
FOX_HEADS = 16
DIFF_HEADS = 8
N_EXPERTS = 16
N_GROUPS = 4
EXPERTS_PER_GROUP = N_EXPERTS // N_GROUPS
TOP_K = 2
D_EXPERT = 512
DEEPNORM_ALPHA = (2 * DEPTH) ** 0.25
LN_EPS = 1e-5
NEG_INF = -1e30

VMEM_LIMIT = 56 * 1024 * 1024

TM_PROJ = 512
TN_PROJ = 512
TQ = 512
TK = 512
TM_EXPERT = 512
TM_COMBINE = 256


def _params(*sem):
    return pltpu.CompilerParams(dimension_semantics=sem, vmem_limit_bytes=VMEM_LIMIT)


def _ada_kernel(c_ref, w_ref, b_ref, o_ref):
    c = c_ref[...]
    sc = c * (1.0 / (1.0 + jnp.exp(-c)))
    o_ref[0] = jnp.dot(sc, w_ref[0], precision=HIGHEST, preferred_element_type=F32) + b_ref[0]


def _ada(c, w, b, tn):
    n_layers, d, n = w.shape
    bsz = c.shape[0]
    return pl.pallas_call(
        _ada_kernel,
        out_shape=jax.ShapeDtypeStruct((n_layers, bsz, n), F32),
        grid=(n_layers, n // tn),
        in_specs=[
            pl.BlockSpec((bsz, d), lambda l, j: (0, 0)),
            pl.BlockSpec((1, d, tn), lambda l, j: (l, 0, j)),
            pl.BlockSpec((1, 1, tn), lambda l, j: (l, 0, j)),
        ],
        out_specs=pl.BlockSpec((1, bsz, tn), lambda l, j: (l, 0, j)),
        compiler_params=_params("parallel", "parallel"),
        name="ada_params",
    )(c, w, b.reshape(n_layers, 1, n))


def _modulated_bf16(x_ref, sh_ref, sc_ref):
    return (x_ref[...] * (1.0 + sc_ref[0]) + sh_ref[0]).astype(BF16)


def _project_chunks(hb, w_ref, o_ref):
    n = w_ref.shape[1]
    for j in range(n // TN_PROJ):
        cols = slice(j * TN_PROJ, (j + 1) * TN_PROJ)
        o_ref[:, cols] = jnp.dot(hb, w_ref[:, cols], preferred_element_type=F32).astype(o_ref.dtype)


def _modproj_kernel(x_ref, sh_ref, sc_ref, w_ref, o_ref):
    _project_chunks(_modulated_bf16(x_ref, sh_ref, sc_ref), w_ref, o_ref)


def _modproj_fox_kernel(x_ref, sh_ref, sc_ref, w_ref, wf_ref, bf_ref, o_ref, cum_ref, carry_ref,
                        *, tiles_per_seq):
    i = pl.program_id(0)

    @pl.when(i % tiles_per_seq == 0)
    def _():
        carry_ref[...] = jnp.zeros_like(carry_ref)

    hb = _modulated_bf16(x_ref, sh_ref, sc_ref)
    _project_chunks(hb, w_ref, o_ref)
    z = jnp.dot(hb, wf_ref[...], preferred_element_type=F32) + bf_ref[...]
    logf = jnp.minimum(z, 0.0) - jnp.log(1.0 + jnp.exp(-jnp.abs(z)))
    tm = z.shape[0]
    lower = (lax.broadcasted_iota(jnp.int32, (tm, tm), 0)
             >= lax.broadcasted_iota(jnp.int32, (tm, tm), 1)).astype(F32)
    cum = jnp.dot(lower, logf, precision=HIGHEST, preferred_element_type=F32) + carry_ref[...]
    cum_ref[...] = cum
    carry_ref[...] = cum[tm - 1:tm, :]


def _modproj(x, shift, scale, w, seq):
    t, d = x.shape
    n = w.shape[1]
    tiles_per_seq = seq // TM_PROJ
    mod_spec = pl.BlockSpec((1, 1, d), lambda i: (i // tiles_per_seq, 0, 0))
    return pl.pallas_call(
        _modproj_kernel,
        out_shape=jax.ShapeDtypeStruct((t, n), BF16),
        grid=(t // TM_PROJ,),
        in_specs=[
            pl.BlockSpec((TM_PROJ, d), lambda i: (i, 0)),
            mod_spec, mod_spec,
            pl.BlockSpec((d, n), lambda i: (0, 0)),
        ],
        out_specs=pl.BlockSpec((TM_PROJ, n), lambda i: (i, 0)),
        compiler_params=_params("parallel"),
        name="modproj",
    )(x, shift, scale, w)


def _modproj_fox(x, shift, scale, w, wf, bf, seq):
    t, d = x.shape
    n = w.shape[1]
    tiles_per_seq = seq // TM_PROJ
    mod_spec = pl.BlockSpec((1, 1, d), lambda i: (i // tiles_per_seq, 0, 0))
    return pl.pallas_call(
        functools.partial(_modproj_fox_kernel, tiles_per_seq=tiles_per_seq),
        out_shape=(jax.ShapeDtypeStruct((t, n), BF16),
                   jax.ShapeDtypeStruct((t, HEAD_BLOCK), F32)),
        grid=(t // TM_PROJ,),
        in_specs=[
            pl.BlockSpec((TM_PROJ, d), lambda i: (i, 0)),
            mod_spec, mod_spec,
            pl.BlockSpec((d, n), lambda i: (0, 0)),
            pl.BlockSpec((d, HEAD_BLOCK), lambda i: (0, 0)),
            pl.BlockSpec((1, HEAD_BLOCK), lambda i: (0, 0)),
        ],
        out_specs=(pl.BlockSpec((TM_PROJ, n), lambda i: (i, 0)),
                   pl.BlockSpec((TM_PROJ, HEAD_BLOCK), lambda i: (i, 0))),
        scratch_shapes=[pltpu.VMEM((1, HEAD_BLOCK), F32)],
        compiler_params=_params("arbitrary"),
        name="modproj_fox",
    )(x, shift, scale, w, wf, bf)


def _attn_kernel(*refs, mode, lam_init, n_heads):
    if mode == "fox":
        q_ref, k_ref, v_ref, cumc_ref, cumr_ref, o_ref, m_ref, l_ref, acc_ref = refs
    else:
        q_ref, k_ref, v_ref, lam_ref, g_ref, o_ref, m_ref, l_ref, acc_ref = refs
    hp = pl.program_id(1)
    i = pl.program_id(2)
    tq = q_ref.shape[0]

    lane = lax.broadcasted_iota(jnp.int32, (1, HEAD_BLOCK), 1)
    lo = lane < HALF
    qs = q_ref[...] * jnp.asarray(HALF ** -0.5, BF16)
    zero = jnp.zeros_like(qs)
    q_maps = (jnp.where(lo, qs, zero), jnp.where(lo, zero, qs))

    if mode == "fox":
        cum_blk = cumc_ref[...]
        cols = tuple(jnp.sum(jnp.where(lane == 2 * hp + a, cum_blk, 0.0), axis=-1, keepdims=True)
                     for a in range(2))
    else:
        slope = jnp.exp2(jnp.full((1, 1), -8.0, F32) * (hp + 1).astype(F32) / n_heads)
        qpos = (i * tq + lax.broadcasted_iota(jnp.int32, (tq, 1), 0)).astype(F32)
        cols = (-slope * qpos,) * 2

    m_ref[...] = jnp.full_like(m_ref, NEG_INF)
    l_ref[...] = jnp.zeros_like(l_ref)
    acc_ref[...] = jnp.zeros_like(acc_ref)

    def process(j, masked):
        k_blk = k_ref[pl.ds(pl.multiple_of(j * TK, TK), TK), :]
        v_blk = v_ref[pl.ds(pl.multiple_of(j * TK, TK), TK), :]
        if mode == "fox":
            rows = tuple(-cumr_ref[0, 0, a:a + 1, pl.ds(pl.multiple_of(j * TK, TK), TK)]
                         for a in range(2))
        else:
            kpos = (j * TK + lax.broadcasted_iota(jnp.int32, (1, TK), 1)).astype(F32)
            rows = (slope * kpos,) * 2
        if masked:
            keep = ((j * TK + lax.broadcasted_iota(jnp.int32, (tq, TK), 1))
                    <= (i * tq + lax.broadcasted_iota(jnp.int32, (tq, TK), 0)))
        for a in range(2):
            s = lax.dot_general(q_maps[a], k_blk, (((1,), (1,)), ((), ())),
                                preferred_element_type=F32)
            s = s + cols[a] + rows[a]
            if masked:
                s = jnp.where(keep, s, NEG_INF)
            m_old = m_ref[a]
            m_new = jnp.maximum(m_old, jnp.max(s, axis=-1, keepdims=True))
            alpha = jnp.exp(m_old - m_new)
            p = jnp.exp(s - m_new)
            l_ref[a] = alpha * l_ref[a] + jnp.sum(p, axis=-1, keepdims=True)
            acc_ref[a] = alpha * acc_ref[a] + jnp.dot(p.astype(BF16), v_blk,
                                                      preferred_element_type=F32)
            m_ref[a] = m_new

    n_full = (i * tq) // TK

    def body(j, carry):
        process(j, False)
        return carry

    lax.fori_loop(0, n_full, body, 0)
    for d in range(tq // TK):
        process(n_full + d, True)

    o_a = acc_ref[0] / l_ref[0]
    o_b = acc_ref[1] / l_ref[1]
    if mode == "fox":
        o_ref[...] = jnp.where(lo, o_a, o_b).astype(o_ref.dtype)
    else:
        lp = lam_ref[0]
        lam = (jnp.exp(jnp.sum(lp[0:1] * lp[1:2], axis=-1, keepdims=True))
               - jnp.exp(jnp.sum(lp[2:3] * lp[3:4], axis=-1, keepdims=True)) + lam_init)
        o = o_a - lam * o_b
        o = o * lax.rsqrt(jnp.mean(o * o, axis=-1, keepdims=True) + LN_EPS)
        o_ref[...] = (o * g_ref[...] * (1.0 - lam_init)).astype(o_ref.dtype)


def _attention(mode, q_arr, kv_arr, k_off, v_off, extras, extra_specs, batch, seq, lam_init=0.0,
               n_heads=DIFF_HEADS):
    nq = seq // TQ
    t = batch * seq
    in_specs = [
        pl.BlockSpec((TQ, HEAD_BLOCK), lambda b, h, i: (b * nq + i, h)),
        pl.BlockSpec((seq, HEAD_BLOCK), lambda b, h, i: (b, k_off + h)),
        pl.BlockSpec((seq, HEAD_BLOCK), lambda b, h, i: (b, v_off + h)),
    ] + extra_specs
    return pl.pallas_call(
        functools.partial(_attn_kernel, mode=mode, lam_init=lam_init, n_heads=n_heads),
        out_shape=jax.ShapeDtypeStruct((t, D_MODEL), BF16),
        grid=(batch, N_HEAD_BLOCKS, nq),
        in_specs=in_specs,
        out_specs=pl.BlockSpec((TQ, HEAD_BLOCK), lambda b, h, i: (b * nq + i, h)),
        scratch_shapes=[pltpu.VMEM((2, TQ, 1), F32), pltpu.VMEM((2, TQ, 1), F32),
                        pltpu.VMEM((2, TQ, HEAD_BLOCK), F32)],
        compiler_params=_params("parallel", "parallel", "arbitrary"),
        name="attn_" + mode,
    )(q_arr, kv_arr, kv_arr, *extras)


def _layer_norm(z, g, b):
    mu = jnp.mean(z, axis=-1, keepdims=True)
    zc = z - mu
    var = jnp.mean(zc * zc, axis=-1, keepdims=True)
    return zc * lax.rsqrt(var + LN_EPS) * g + b


def _argmax_first(vals):
    best, idx = vals[0], jnp.zeros(vals[0].shape, jnp.int32)
    for j in range(1, len(vals)):
        better = vals[j] > best
        idx = jnp.where(better, j, idx)
        best = jnp.where(better, vals[j], best)
    return idx


def _select(vals, idx):
    out = vals[0]
    for j in range(1, len(vals)):
        out = jnp.where(idx == j, vals[j], out)
    return out


def _post_attn_kernel(o_ref, x_ref, wo_ref, ga_ref, lng_ref, lnb_ref, shf_ref, scf_ref,
                      rwt_ref, rb_ref, xo_ref, h_ref, eidx_ref, wts_ref):
    y = jnp.dot(o_ref[...], wo_ref[...], preferred_element_type=F32)
    z = DEEPNORM_ALPHA * x_ref[...] + (1.0 + ga_ref[0]) * y
    xn = _layer_norm(z, lng_ref[...], lnb_ref[...])
    xo_ref[...] = xn
    h = xn * (1.0 + scf_ref[0]) + shf_ref[0]
    h_ref[...] = h

    logits = lax.dot_general(rwt_ref[...], h, (((1,), (1,)), ((), ())),
                             precision=HIGHEST, preferred_element_type=F32)
    rows = [logits[e:e + 1, :] for e in range(N_EXPERTS)]
    mx = functools.reduce(jnp.maximum, rows)
    ex = [jnp.exp(r - mx) for r in rows]
    den = functools.reduce(lambda a, b: a + b, ex)
    probs = [e_ / den for e_ in ex]
    sel = [probs[e] + rb_ref[e:e + 1, :] for e in range(N_EXPERTS)]

    grp_scores = []
    for g in range(N_GROUPS):
        v = sel[g * EXPERTS_PER_GROUP:(g + 1) * EXPERTS_PER_GROUP]
        pair_sums = [v[a] + v[b] for a in range(EXPERTS_PER_GROUP) for b in range(a + 1, EXPERTS_PER_GROUP)]
        grp_scores.append(functools.reduce(jnp.maximum, pair_sums))
    g_idx = _argmax_first(grp_scores)

    in_sel = [_select([sel[g * EXPERTS_PER_GROUP + j] for g in range(N_GROUPS)], g_idx)
              for j in range(EXPERTS_PER_GROUP)]
    in_prob = [_select([probs[g * EXPERTS_PER_GROUP + j] for g in range(N_GROUPS)], g_idx)
               for j in range(EXPERTS_PER_GROUP)]
    i1 = _argmax_first(in_sel)
    b2 = jnp.full(in_sel[0].shape, -jnp.inf, F32)
    i2 = jnp.zeros(in_sel[0].shape, jnp.int32)
    for j in range(EXPERTS_PER_GROUP):
        cand = (i1 != j) & (in_sel[j] > b2)
        i2 = jnp.where(cand, j, i2)
        b2 = jnp.where(cand, in_sel[j], b2)
    p1 = _select(in_prob, i1)
    p2 = _select(in_prob, i2)
    tot = p1 + p2
    eidx_ref[0:1, :] = g_idx * EXPERTS_PER_GROUP + i1
    eidx_ref[1:2, :] = g_idx * EXPERTS_PER_GROUP + i2
    wts_ref[0:1, :] = p1 / tot
    wts_ref[1:2, :] = p2 / tot


def _post_attn(o, x, wo, g_a, ln_g, ln_b, sh_f, sc_f, router_wt, router_b, seq):
    t, d = x.shape
    tm = TM_PROJ
    tiles_per_seq = seq // tm
    mod_spec = pl.BlockSpec((1, 1, d), lambda i: (i // tiles_per_seq, 0, 0))
    vec_spec = pl.BlockSpec((1, d), lambda i: (0, 0))
    tok_spec = pl.BlockSpec((tm, d), lambda i: (i, 0))
    route_spec = pl.BlockSpec((TOP_K, tm), lambda i: (0, i))
    return pl.pallas_call(
        _post_attn_kernel,
        out_shape=(jax.ShapeDtypeStruct((t, d), F32), jax.ShapeDtypeStruct((t, d), F32),
                   jax.ShapeDtypeStruct((TOP_K, t), jnp.int32), jax.ShapeDtypeStruct((TOP_K, t), F32)),
        grid=(t // tm,),
        in_specs=[
            tok_spec, tok_spec,
            pl.BlockSpec((d, d), lambda i: (0, 0)),
            mod_spec, vec_spec, vec_spec, mod_spec, mod_spec,
            pl.BlockSpec((N_EXPERTS, d), lambda i: (0, 0)),
            pl.BlockSpec((N_EXPERTS, 1), lambda i: (0, 0)),
        ],
        out_specs=(tok_spec, tok_spec, route_spec, route_spec),
        compiler_params=_params("parallel"),
        name="post_attn",
    )(o, x, wo, g_a, ln_g, ln_b, sh_f, sc_f, router_wt, router_b)


def _routing_tables(eidx, rows_pad):
    t = eidx.shape[1]
    e_flat = eidx.reshape(-1)
    onehot = (e_flat[:, None] == jnp.arange(N_EXPERTS, dtype=jnp.int32)[None, :]).astype(jnp.int32)
    csum = jnp.cumsum(onehot, axis=0)
    rank = jnp.sum(csum * onehot, axis=1) - 1
    counts = csum[-1]
    padded = ((counts + TM_EXPERT - 1) // TM_EXPERT) * TM_EXPERT
    ends = jnp.cumsum(padded)
    starts = ends - padded
    pos = jnp.sum(onehot * starts[None, :], axis=1) + rank
    tok = jnp.tile(jnp.arange(t, dtype=jnp.int32), TOP_K)
    sorted_tok = jnp.zeros((rows_pad,), jnp.int32).at[pos].set(tok)
    n_tiles = rows_pad // TM_EXPERT
    tile_start = jnp.arange(n_tiles, dtype=jnp.int32) * TM_EXPERT
    tile_expert = jnp.minimum(jnp.sum((tile_start[:, None] >= ends[None, :]).astype(jnp.int32), axis=1),
                              N_EXPERTS - 1).astype(jnp.int32)
    n_valid = (ends[-1] // TM_EXPERT).astype(jnp.int32).reshape(1)
    return pos.astype(jnp.int32), sorted_tok, tile_expert, n_valid


def _expert_kernel(te_ref, nv_ref, tok_ref, h_hbm, wg_ref, wu_ref, wd_ref, y_ref, buf, sem):
    i = pl.program_id(0)
    nv = nv_ref[0]
    slot = i % 2

    def issue(tile, slot_):
        base = tile * TM_EXPERT

        def body(r, carry):
            tok = tok_ref[base + r]
            pltpu.make_async_copy(h_hbm.at[pl.ds(tok, 1)], buf.at[slot_, pl.ds(r, 1)],
                                  sem.at[slot_]).start()
            return carry

        lax.fori_loop(0, TM_EXPERT, body, 0, unroll=8)

    @pl.when((i == 0) & (nv > 0))
    def _():
        issue(0, 0)

    @pl.when(i + 1 < nv)
    def _():
        issue(i + 1, 1 - slot)

    @pl.when(i < nv)
    def _():
        pltpu.make_async_copy(h_hbm.at[pl.ds(0, TM_EXPERT)], buf.at[slot], sem.at[slot]).wait()
        hb = buf[slot].astype(BF16)
        g = jnp.dot(hb, wg_ref[0], preferred_element_type=F32)
        u = jnp.dot(hb, wu_ref[0], preferred_element_type=F32)
        a = g * (1.0 / (1.0 + jnp.exp(-g))) * u
        y_ref[...] = jnp.dot(a.astype(BF16), wd_ref[0], preferred_element_type=F32)

    @pl.when(i >= nv)
    def _():
        y_ref[...] = jnp.zeros_like(y_ref)


def _experts(h, tile_expert, n_valid, sorted_tok, wg, wu, wd):
    t, d = h.shape
    rows_pad = sorted_tok.shape[0]
    n_tiles = rows_pad // TM_EXPERT
    grid_spec = pltpu.PrefetchScalarGridSpec(
        num_scalar_prefetch=3,
        grid=(n_tiles,),
        in_specs=[
            pl.BlockSpec(memory_space=pl.ANY),
            pl.BlockSpec((1, d, D_EXPERT), lambda i, te, nv, tok: (te[i], 0, 0)),
            pl.BlockSpec((1, d, D_EXPERT), lambda i, te, nv, tok: (te[i], 0, 0)),
            pl.BlockSpec((1, D_EXPERT, d), lambda i, te, nv, tok: (te[i], 0, 0)),
        ],
        out_specs=pl.BlockSpec((TM_EXPERT, d), lambda i, te, nv, tok: (i, 0)),
        scratch_shapes=[pltpu.VMEM((2, TM_EXPERT, d), F32), pltpu.SemaphoreType.DMA((2,))],
    )
    return pl.pallas_call(
        _expert_kernel,
        out_shape=jax.ShapeDtypeStruct((rows_pad, d), F32),
        grid_spec=grid_spec,
        compiler_params=_params("arbitrary"),
        name="experts",
    )(tile_expert, n_valid, sorted_tok, h, wg, wu, wd)


def _combine_kernel(pos_ref, y_hbm, x_ref, w_ref, gf_ref, lng_ref, lnb_ref, o_ref, buf, sem, *, n_tok):
    i = pl.program_id(0)
    n = pl.num_programs(0)
    slot = i % 2
    tm = x_ref.shape[0]

    def issue(tile, slot_):
        base = tile * tm

        def body(r, carry):
            for k in range(TOP_K):
                p = pos_ref[k * n_tok + base + r]
                pltpu.make_async_copy(y_hbm.at[pl.ds(p, 1)], buf.at[slot_, k, pl.ds(r, 1)],
                                      sem.at[slot_]).start()
            return carry

        lax.fori_loop(0, tm, body, 0, unroll=8)

    @pl.when(i == 0)
    def _():
        issue(0, 0)

    @pl.when(i + 1 < n)
    def _():
        issue(i + 1, 1 - slot)

    for k in range(TOP_K):
        pltpu.make_async_copy(y_hbm.at[pl.ds(0, tm)], buf.at[slot, k], sem.at[slot]).wait()
    w = w_ref[...]
    m = w[:, 0:1] * buf[slot, 0] + w[:, 1:2] * buf[slot, 1]
    z = DEEPNORM_ALPHA * x_ref[...] + (1.0 + gf_ref[0]) * m
    o_ref[...] = _layer_norm(z, lng_ref[...], lnb_ref[...])


def _combine(pos, y, x, wts_t, g_f, ln_g, ln_b, seq):
    t, d = x.shape
    tm = TM_COMBINE
    tiles_per_seq = seq // tm
    grid_spec = pltpu.PrefetchScalarGridSpec(
        num_scalar_prefetch=1,
        grid=(t // tm,),
        in_specs=[
            pl.BlockSpec(memory_space=pl.ANY),
            pl.BlockSpec((tm, d), lambda i, pos_: (i, 0)),
            pl.BlockSpec((tm, TOP_K), lambda i, pos_: (i, 0)),
            pl.BlockSpec((1, 1, d), lambda i, pos_: (i // tiles_per_seq, 0, 0)),
            pl.BlockSpec((1, d), lambda i, pos_: (0, 0)),
            pl.BlockSpec((1, d), lambda i, pos_: (0, 0)),
        ],
        out_specs=pl.BlockSpec((tm, d), lambda i, pos_: (i, 0)),
        scratch_shapes=[pltpu.VMEM((2, TOP_K, tm, d), F32), pltpu.SemaphoreType.DMA((2,))],
    )
    return pl.pallas_call(
        functools.partial(_combine_kernel, n_tok=t),
        out_shape=jax.ShapeDtypeStruct((t, d), F32),
        grid_spec=grid_spec,
        compiler_params=_params("arbitrary"),
        name="combine",
    )(pos, y, x, wts_t, g_f, ln_g, ln_b)


def kernel(x, c, ada_w, ada_b, ln_attn_g, ln_attn_b, ln_ffn_g, ln_ffn_b, fox_w_in, fox_b_f, fox_w_o,
           kv_ada_w, kv_ada_b, kv_w, diff_w_q, diff_lambda, diff_norm_g, diff_w_o, router_w, router_b,
           moe_w_gate, moe_w_up, moe_w_down):
    bsz, seq, d = x.shape
    t = bsz * seq
    rows_pad = TOP_K * t + N_EXPERTS * TM_EXPERT
    nq = seq // TQ

    mod = _ada(c, ada_w, ada_b, 1536)
    kv_mod = _ada(c, kv_ada_w[None], kv_ada_b[None], 1024)[0]

    def mod_part(l, k):
        return mod[l, :, k * d:(k + 1) * d].reshape(bsz, 1, d)

    router_wt = router_w.T
    router_bc = router_b.reshape(N_EXPERTS, 1)
    xt = x.reshape(t, d)

    kv = None
    for l in range(DEPTH):
        sh_a, sc_a, g_a, sh_f, sc_f, g_f = (mod_part(l, k) for k in range(6))
        if l < N_A:
            w_in = fox_w_in[l]
            w_qkv = w_in[:, :3 * d].astype(BF16)
            w_f = jnp.pad(w_in[:, 3 * d:], ((0, 0), (0, HEAD_BLOCK - FOX_HEADS))).astype(BF16)
            b_f = jnp.pad(fox_b_f[l], (0, HEAD_BLOCK - FOX_HEADS)).reshape(1, HEAD_BLOCK)
            qkv, cum = _modproj_fox(xt, sh_a, sc_a, w_qkv, w_f, b_f, seq)
            cum_rows = cum[:, :FOX_HEADS].reshape(bsz, seq, N_HEAD_BLOCKS, 2).transpose(0, 2, 3, 1)
            o = _attention(
                "fox", qkv, qkv, N_HEAD_BLOCKS, 2 * N_HEAD_BLOCKS, (cum, cum_rows),
                [pl.BlockSpec((TQ, HEAD_BLOCK), lambda b, h, i: (b * nq + i, 0)),
                 pl.BlockSpec((1, 1, 2, seq), lambda b, h, i: (b, h, 0, 0))],
                bsz, seq)
            w_o = fox_w_o[l].astype(BF16)
        else:
            j = l - N_A
            if l == N_A:
                kv = _modproj(xt, kv_mod[:, :d].reshape(bsz, 1, d), kv_mod[:, d:].reshape(bsz, 1, d),
                              kv_w.astype(BF16), seq)
            qd = _modproj(xt, sh_a, sc_a, diff_w_q[j].astype(BF16), seq)
            lam_init = 0.8 - 0.6 * math.exp(-0.3 * l)
            o = _attention(
                "diff", qd, kv, 0, N_HEAD_BLOCKS,
                (diff_lambda[j][None], diff_norm_g[j].reshape(1, HEAD_BLOCK)),
                [pl.BlockSpec((1, 4, HALF), lambda b, h, i: (0, 0, 0)),
                 pl.BlockSpec((1, HEAD_BLOCK), lambda b, h, i: (0, 0))],
                bsz, seq, lam_init=lam_init)
            w_o = diff_w_o[j].astype(BF16)

        xt, h, eidx, wts = _post_attn(o, xt, w_o, g_a, ln_attn_g[l].reshape(1, d),
                                      ln_attn_b[l].reshape(1, d), sh_f, sc_f, router_wt, router_bc, seq)
        pos, sorted_tok, tile_expert, n_valid = _routing_tables(eidx, rows_pad)
        y = _experts(h, tile_expert, n_valid, sorted_tok, moe_w_gate[l].astype(BF16),
                     moe_w_up[l].astype(BF16), moe_w_down[l].astype(BF16))
        xt = _combine(pos, y, xt, wts.T, g_f, ln_ffn_g[l].reshape(1, d), ln_ffn_b[l].reshape(1, d), seq)
    return xt.reshape(bsz, seq, d)
```

```python
import functools
import math

import jax
import jax.numpy as jnp
from jax import lax
from jax.experimental import pallas as pl
from jax.experimental.pallas import tpu as pltpu

F32 = jnp.float32
BF16 = jnp.bfloat16
HIGHEST = lax.Precision.HIGHEST

D_MODEL = 1024
DEPTH = 4
N_A = DEPTH // 2
HEAD_BLOCK = 128
N_HEAD_BLOCKS = D_MODEL // HEAD_BLOCK
HALF = HEAD_BLOCK // 2
FOX_HEADS = 16
DIFF_HEADS = 8
N_EXPERTS = 16
N_GROUPS = 4
EXPERTS_PER_GROUP = N_EXPERTS // N_GROUPS
TOP_K = 2
D_EXPERT = 512
DEEPNORM_ALPHA = (2 * DEPTH) ** 0.25
LN_EPS = 1e-5
NEG_INF = -1e30
LOG2E = math.log2(math.e)

VMEM_LIMIT = 56 * 1024 * 1024

TM_PROJ = 512
TN_PROJ = 256
TQ = 512
TK = 512
BIAS_ROWS = 16
ONES_ROWS = 16
TM_EXPERT = 512
TM_COMBINE = 256

NT_DIMS = (((1,), (1,)), ((), ()))


def _params(*sem):
    return pltpu.CompilerParams(dimension_semantics=sem, vmem_limit_bytes=VMEM_LIMIT)


def _split3(v):
    hi = v.astype(BF16).astype(F32)
    r = v - hi
    mid = r.astype(BF16).astype(F32)
    lo = (r - mid).astype(BF16).astype(F32)
    return hi, mid, lo


def _ada_kernel(c_ref, w_ref, b_ref, o_ref):
    c = c_ref[...]
    sc = c * (1.0 / (1.0 + jnp.exp(-c)))
    o_ref[0] = jnp.dot(sc, w_ref[0], precision=HIGHEST, preferred_element_type=F32) + b_ref[0]


def _ada(c, w, b, tn):
    n_layers, d, n = w.shape
    bsz = c.shape[0]
    return pl.pallas_call(
        _ada_kernel,
        out_shape=jax.ShapeDtypeStruct((n_layers, bsz, n), F32),
        grid=(n_layers, n // tn),
        in_specs=[
            pl.BlockSpec((bsz, d), lambda l, j: (0, 0)),
            pl.BlockSpec((1, d, tn), lambda l, j: (l, 0, j)),
            pl.BlockSpec((1, 1, tn), lambda l, j: (l, 0, j)),
        ],
        out_specs=pl.BlockSpec((1, bsz, tn), lambda l, j: (l, 0, j)),
        compiler_params=_params("parallel", "parallel"),
        name="ada_params",
    )(c, w, b.reshape(n_layers, 1, n))


def _modproj_kernel(*refs, n_std, n_t, with_forget, tiles_per_seq):
    x_ref, sh_ref, sc_ref = refs[:3]
    pos = 3
    wstd = refs[pos:pos + n_std]; pos += n_std
    wt = refs[pos:pos + n_t]; pos += n_t
    if with_forget:
        wf_ref, bf_ref = refs[pos:pos + 2]; pos += 2
    ostd = refs[pos:pos + n_std]; pos += n_std
    ot = refs[pos:pos + n_t]; pos += n_t
    if with_forget:
        cum_ref, cumt_ref, carry_ref = refs[pos:pos + 3]

    hb = (x_ref[...] * (1.0 + sc_ref[0]) + sh_ref[0]).astype(BF16)
    for w_ref, o_ref in zip(wstd, ostd):
        for j in range(w_ref.shape[1] // TN_PROJ):
            cols = slice(j * TN_PROJ, (j + 1) * TN_PROJ)
            o_ref[:, cols] = jnp.dot(hb, w_ref[:, cols], preferred_element_type=F32).astype(o_ref.dtype)
    for w_ref, o_ref in zip(wt, ot):
        for j in range(w_ref.shape[0] // TN_PROJ):
            rows = slice(j * TN_PROJ, (j + 1) * TN_PROJ)
            o_ref[rows, :] = lax.dot_general(w_ref[rows, :], hb, NT_DIMS,
                                             preferred_element_type=F32).astype(o_ref.dtype)

    if with_forget:
        @pl.when(pl.program_id(0) % tiles_per_seq == 0)
        def _():
            carry_ref[...] = jnp.zeros_like(carry_ref)

        z = jnp.dot(hb, wf_ref[...], preferred_element_type=F32) + bf_ref[...]
        logf = jnp.minimum(z, 0.0) - jnp.log(1.0 + jnp.exp(-jnp.abs(z)))
        tm = z.shape[0]
        lower = (lax.broadcasted_iota(jnp.int32, (tm, tm), 0)
                 >= lax.broadcasted_iota(jnp.int32, (tm, tm), 1)).astype(F32)
        cum = jnp.dot(lower, logf, precision=HIGHEST, preferred_element_type=F32) + carry_ref[...]
        cum_ref[...] = cum
        cumt_ref[...] = cum.T[:FOX_HEADS, :]
        carry_ref[...] = cum[tm - 1:tm, :]


def _modproj(x, shift, scale, w_std, w_t, seq, forget=None):
    t, d = x.shape
    tm = TM_PROJ
    tiles_per_seq = seq // tm
    const = lambda i: (0, 0)
    mod_spec = pl.BlockSpec((1, 1, d), lambda i: (i // tiles_per_seq, 0, 0))
    in_specs = [pl.BlockSpec((tm, d), lambda i: (i, 0)), mod_spec, mod_spec]
    in_specs += [pl.BlockSpec(w.shape, const) for w in w_std]
    in_specs += [pl.BlockSpec(w.shape, const) for w in w_t]
    out_shape = [jax.ShapeDtypeStruct((t, w.shape[1]), BF16) for w in w_std]
    out_shape += [jax.ShapeDtypeStruct((w.shape[0], t), BF16) for w in w_t]
    out_specs = [pl.BlockSpec((tm, w.shape[1]), lambda i: (i, 0)) for w in w_std]
    out_specs += [pl.BlockSpec((w.shape[0], tm), lambda i: (0, i)) for w in w_t]
    args = [x, shift, scale, *w_std, *w_t]
    scratch = []
    if forget is not None:
        wf, bf = forget
        in_specs += [pl.BlockSpec(wf.shape, const), pl.BlockSpec(bf.shape, const)]
        args += [wf, bf]
        out_shape += [jax.ShapeDtypeStruct((t, HEAD_BLOCK), F32), jax.ShapeDtypeStruct((FOX_HEADS, t), F32)]
        out_specs += [pl.BlockSpec((tm, HEAD_BLOCK), lambda i: (i, 0)),
                      pl.BlockSpec((FOX_HEADS, tm), lambda i: (0, i))]
        scratch = [pltpu.VMEM((1, HEAD_BLOCK), F32)]
    return pl.pallas_call(
        functools.partial(_modproj_kernel, n_std=len(w_std), n_t=len(w_t),
                          with_forget=forget is not None, tiles_per_seq=tiles_per_seq),
        out_shape=tuple(out_shape),
        grid=(t // tm,),
        in_specs=in_specs,
        out_specs=tuple(out_specs),
        scratch_shapes=scratch,
        compiler_params=_params("arbitrary"),
        name="modproj_fox" if forget is not None else "modproj",
    )(*args)


def _bias_rows(entries, tq):
    rid = lax.broadcasted_iota(jnp.int32, (BIAS_ROWS, tq), 0)
    out = jnp.zeros((BIAS_ROWS, tq), F32)
    for r, v in entries.items():
        out = jnp.where(rid == r, v, out)
    return out


def _attn_kernel(*refs, mode, lam_init, n_heads):
    if mode == "fox":
        q_ref, k_ref, kb_ref, v_ref, cumt_ref, o_ref, m_ref, acc_ref = refs
    else:
        q_ref, k_ref, kb_ref, v_ref, lam_ref, g_ref, o_ref, m_ref, acc_ref = refs
    hp = pl.program_id(1)
    i = pl.program_id(2)
    tq = q_ref.shape[1]

    qs = q_ref[...] * jnp.asarray(HALF ** -0.5 * LOG2E, BF16)
    zeros_half = jnp.zeros((HALF, tq), BF16)
    q_tops = (jnp.concatenate([qs[:HALF], zeros_half], axis=0),
              jnp.concatenate([zeros_half, qs[HALF:]], axis=0))
    pad = jnp.zeros((HEAD_BLOCK - BIAS_ROWS, tq), BF16)
    if mode == "fox":
        q_bias = []
        for a in range(2):
            c = _split3(cumt_ref[0, a:a + 1, :] * LOG2E)
            base = 6 * a
            rows = {base: 1.0, base + 1: 1.0, base + 2: 1.0,
                    base + 3: c[0], base + 4: c[1], base + 5: c[2]}
            q_bias.append(_bias_rows(rows, tq).astype(BF16))
    else:
        slope = jnp.exp2(jnp.full((1, 1), -8.0, F32) * (hp + 1).astype(F32) / n_heads)
        l3 = _split3(slope * LOG2E)
        qpos = (i * tq + lax.broadcasted_iota(jnp.int32, (1, tq), 1)).astype(F32)
        t3 = _split3(-(slope * LOG2E) * qpos)
        rows = {0: l3[0], 1: l3[0], 2: l3[1], 3: l3[1], 4: l3[2], 5: l3[2],
                6: t3[0], 7: t3[1], 8: t3[2]}
        q_bias = [_bias_rows(rows, tq).astype(BF16)] * 2
    q_aug = tuple(jnp.concatenate([q_tops[a], q_bias[a], pad], axis=0) for a in range(2))

    m_ref[...] = jnp.full_like(m_ref, NEG_INF)
    acc_ref[...] = jnp.zeros_like(acc_ref)
    ones_rows = jnp.ones((ONES_ROWS, TK), BF16)

    def process(j, masked):
        ks = pl.ds(pl.multiple_of(j * TK, TK), TK)
        if mode == "fox":
            kb_blk = kb_ref[0, 0, ks, :]
        else:
            kb_blk = kb_ref[ks, :]
        k_aug = jnp.concatenate([k_ref[ks, :], kb_blk], axis=1)
        v_aug = jnp.concatenate([v_ref[:, ks], ones_rows], axis=0)
        if masked:
            keep = ((j * TK + lax.broadcasted_iota(jnp.int32, (TK, tq), 0))
                    <= (i * tq + lax.broadcasted_iota(jnp.int32, (TK, tq), 1)))
        for a in range(2):
            s = jnp.dot(k_aug, q_aug[a], preferred_element_type=F32)
            if masked:
                s = jnp.where(keep, s, NEG_INF)
            m_old = m_ref[a]
            m_new = jnp.maximum(m_old, jnp.max(s, axis=0, keepdims=True))
            alpha = jnp.exp2(m_old - m_new)
            p = jnp.exp2(s - m_new).astype(BF16)
            acc_ref[a] = alpha * acc_ref[a] + jnp.dot(v_aug, p, preferred_element_type=F32)
            m_ref[a] = m_new

    n_full = (i * tq) // TK

    def body(j, carry):
        process(j, False)
        return carry

    lax.fori_loop(0, n_full, body, 0)
    for d in range(tq // TK):
        process(n_full + d, True)

    inv = [1.0 / acc_ref[a, HEAD_BLOCK:HEAD_BLOCK + 1, :] for a in range(2)]
    if mode == "fox":
        o_t = jnp.concatenate([acc_ref[0, :HALF, :] * inv[0], acc_ref[1, HALF:HEAD_BLOCK, :] * inv[1]],
                              axis=0)
        o_ref[...] = o_t.T.astype(o_ref.dtype)
    else:
        lp = lam_ref[0]
        lam = (jnp.exp(jnp.sum(lp[0:1] * lp[1:2], axis=-1, keepdims=True))
               - jnp.exp(jnp.sum(lp[2:3] * lp[3:4], axis=-1, keepdims=True)) + lam_init)
        o_t = acc_ref[0, :HEAD_BLOCK, :] * inv[0] - lam * (acc_ref[1, :HEAD_BLOCK, :] * inv[1])
        o = o_t.T
        o = o * lax.rsqrt(jnp.mean(o * o, axis=-1, keepdims=True) + LN_EPS)
        o_ref[...] = (o * g_ref[...] * (1.0 - lam_init)).astype(o_ref.dtype)


def _attention(mode, q_t, k_arr, k_off, kb, kb_spec, v_t, v_off, extras, extra_specs, batch, seq,
               lam_init=0.0, n_heads=DIFF_HEADS):
    nq = seq // TQ
    t = batch * seq
    in_specs = [
        pl.BlockSpec((HEAD_BLOCK, TQ), lambda b, h, i: (h, b * nq + i)),
        pl.BlockSpec((seq, HEAD_BLOCK), lambda b, h, i: (b, k_off + h)),
        kb_spec,
        pl.BlockSpec((HEAD_BLOCK, seq), lambda b, h, i: (v_off + h, b)),
    ] + extra_specs
    return pl.pallas_call(
        functools.partial(_attn_kernel, mode=mode, lam_init=lam_init, n_heads=n_heads),
        out_shape=jax.ShapeDtypeStruct((t, D_MODEL), BF16),
        grid=(batch, N_HEAD_BLOCKS, nq),
        in_specs=in_specs,
        out_specs=pl.BlockSpec((TQ, HEAD_BLOCK), lambda b, h, i: (b * nq + i, h)),
        scratch_shapes=[pltpu.VMEM((2, 1, TQ), F32),
                        pltpu.VMEM((2, HEAD_BLOCK + ONES_ROWS, TQ), F32)],
        compiler_params=_params("parallel", "parallel", "arbitrary"),
        name="attn_" + mode,
    )(q_t, k_arr, kb, v_t, *extras)


def _bias_lanes(cols):
    lanes = jnp.stack(cols, axis=-1)
    pad = [(0, 0)] * (lanes.ndim - 1) + [(0, HEAD_BLOCK - lanes.shape[-1])]
    return jnp.pad(lanes, pad).astype(BF16)


def _fox_key_bias(cum, bsz, seq):
    r = (-LOG2E) * cum[:, :FOX_HEADS].reshape(bsz, seq, N_HEAD_BLOCKS, 2).transpose(0, 2, 3, 1)
    one = jnp.ones(r.shape[:2] + r.shape[3:], F32)
    cols = []
    for a in range(2):
        cols += list(_split3(r[:, :, a])) + [one, one, one]
    return _bias_lanes(cols)


def _alibi_key_bias(seq):
    kpos = jnp.arange(seq, dtype=jnp.int32)
    hi = ((kpos // 64) * 64).astype(F32)
    lo = (kpos % 64).astype(F32)
    one = jnp.ones((seq,), F32)
    return _bias_lanes([hi, lo, hi, lo, hi, lo, one, one, one])


def _layer_norm(z, g, b):
    mu = jnp.mean(z, axis=-1, keepdims=True)
    zc = z - mu
    var = jnp.mean(zc * zc, axis=-1, keepdims=True)
    return zc * lax.rsqrt(var + LN_EPS) * g + b


def _argmax_first(vals):
    best, idx = vals[0], jnp.zeros(vals[0].shape, jnp.int32)
    for j in range(1, len(vals)):
        better = vals[j] > best
        idx = jnp.where(better, j, idx)
        best = jnp.where(better, vals[j], best)
    return idx


def _select(vals, idx):
    out = vals[0]
    for j in range(1, len(vals)):
        out = jnp.where(idx == j, vals[j], out)
    return out


def _post_attn_kernel(o_ref, x_ref, wo_ref, ga_ref, lng_ref, lnb_ref, shf_ref, scf_ref,
                      rwt_ref, rb_ref, xo_ref, h_ref, eidx_ref, wts_ref):
    y = jnp.dot(o_ref[...], wo_ref[...], preferred_element_type=F32)
    z = DEEPNORM_ALPHA * x_ref[...] + (1.0 + ga_ref[0]) * y
    xn = _layer_norm(z, lng_ref[...], lnb_ref[...])
    xo_ref[...] = xn
    h = xn * (1.0 + scf_ref[0]) + shf_ref[0]
    h_ref[...] = h

    logits = lax.dot_general(rwt_ref[...], h, NT_DIMS, precision=HIGHEST,
                             preferred_element_type=F32)
    rows = [logits[e:e + 1, :] for e in range(N_EXPERTS)]
    mx = functools.reduce(jnp.maximum, rows)
    ex = [jnp.exp(r - mx) for r in rows]
    den = functools.reduce(lambda a, b: a + b, ex)
    probs = [e_ / den for e_ in ex]
    sel = [probs[e] + rb_ref[e:e + 1, :] for e in range(N_EXPERTS)]

    grp_scores = []
    for g in range(N_GROUPS):
        v = sel[g * EXPERTS_PER_GROUP:(g + 1) * EXPERTS_PER_GROUP]
        pair_sums = [v[a] + v[b] for a in range(EXPERTS_PER_GROUP) for b in range(a + 1, EXPERTS_PER_GROUP)]
        grp_scores.append(functools.reduce(jnp.maximum, pair_sums))
    g_idx = _argmax_first(grp_scores)

    in_sel = [_select([sel[g * EXPERTS_PER_GROUP + j] for g in range(N_GROUPS)], g_idx)
              for j in range(EXPERTS_PER_GROUP)]
    in_prob = [_select([probs[g * EXPERTS_PER_GROUP + j] for g in range(N_GROUPS)], g_idx)
               for j in range(EXPERTS_PER_GROUP)]
    i1 = _argmax_first(in_sel)
    b2 = jnp.full(in_sel[0].shape, -jnp.inf, F32)
    i2 = jnp.zeros(in_sel[0].shape, jnp.int32)
    for j in range(EXPERTS_PER_GROUP):
        cand = (i1 != j) & (in_sel[j] > b2)
        i2 = jnp.where(cand, j, i2)
        b2 = jnp.where(cand, in_sel[j], b2)
    p1 = _select(in_prob, i1)
    p2 = _select(in_prob, i2)
    tot = p1 + p2
    eidx_ref[0:1, :] = g_idx * EXPERTS_PER_GROUP + i1
    eidx_ref[1:2, :] = g_idx * EXPERTS_PER_GROUP + i2
    wts_ref[0:1, :] = p1 / tot
    wts_ref[1:2, :] = p2 / tot


def _post_attn(o, x, wo, g_a, ln_g, ln_b, sh_f, sc_f, router_wt, router_b, seq):
    t, d = x.shape
    tm = TM_PROJ
    tiles_per_seq = seq // tm
    mod_spec = pl.BlockSpec((1, 1, d), lambda i: (i // tiles_per_seq, 0, 0))
    vec_spec = pl.BlockSpec((1, d), lambda i: (0, 0))
    tok_spec = pl.BlockSpec((tm, d), lambda i: (i, 0))
    route_spec = pl.BlockSpec((TOP_K, tm), lambda i: (0, i))
    return pl.pallas_call(
        _post_attn_kernel,
        out_shape=(jax.ShapeDtypeStruct((t, d), F32), jax.ShapeDtypeStruct((t, d), F32),
                   jax.ShapeDtypeStruct((TOP_K, t), jnp.int32), jax.ShapeDtypeStruct((TOP_K, t), F32)),
        grid=(t // tm,),
        in_specs=[
            tok_spec, tok_spec,
            pl.BlockSpec((d, d), lambda i: (0, 0)),
            mod_spec, vec_spec, vec_spec, mod_spec, mod_spec,
            pl.BlockSpec((N_EXPERTS, d), lambda i: (0, 0)),
            pl.BlockSpec((N_EXPERTS, 1), lambda i: (0, 0)),
        ],
        out_specs=(tok_spec, tok_spec, route_spec, route_spec),
        compiler_params=_params("parallel"),
        name="post_attn",
    )(o, x, wo, g_a, ln_g, ln_b, sh_f, sc_f, router_wt, router_b)


def _routing_tables(eidx, rows_pad):
    t = eidx.shape[1]
    e_flat = eidx.reshape(-1)
    onehot = (e_flat[:, None] == jnp.arange(N_EXPERTS, dtype=jnp.int32)[None, :]).astype(jnp.int32)
    csum = jnp.cumsum(onehot, axis=0)
    rank = jnp.sum(csum * onehot, axis=1) - 1
    counts = csum[-1]
    padded = ((counts + TM_EXPERT - 1) // TM_EXPERT) * TM_EXPERT
    ends = jnp.cumsum(padded)
    starts = ends - padded
    pos = jnp.sum(onehot * starts[None, :], axis=1) + rank
    tok = jnp.tile(jnp.arange(t, dtype=jnp.int32), TOP_K)
    sorted_tok = jnp.zeros((rows_pad,), jnp.int32).at[pos].set(tok)
    n_tiles = rows_pad // TM_EXPERT
    tile_start = jnp.arange(n_tiles, dtype=jnp.int32) * TM_EXPERT
    tile_expert = jnp.minimum(jnp.sum((tile_start[:, None] >= ends[None, :]).astype(jnp.int32), axis=1),
                              N_EXPERTS - 1).astype(jnp.int32)
    n_valid = (ends[-1] // TM_EXPERT).astype(jnp.int32).reshape(1)
    return pos.astype(jnp.int32), sorted_tok, tile_expert, n_valid


def _expert_kernel(te_ref, nv_ref, tok_ref, h_hbm, wg_ref, wu_ref, wd_ref, y_ref, buf, sem):
    i = pl.program_id(0)
    nv = nv_ref[0]
    slot = i % 2

    def issue(tile, slot_):
        base = tile * TM_EXPERT

        def body(r, carry):
            tok = tok_ref[base + r]
            pltpu.make_async_copy(h_hbm.at[pl.ds(tok, 1)], buf.at[slot_, pl.ds(r, 1)],
                                  sem.at[slot_]).start()
            return carry

        lax.fori_loop(0, TM_EXPERT, body, 0, unroll=8)

    @pl.when((i == 0) & (nv > 0))
    def _():
        issue(0, 0)

    @pl.when(i + 1 < nv)
    def _():
        issue(i + 1, 1 - slot)

    @pl.when(i < nv)
    def _():
        pltpu.make_async_copy(h_hbm.at[pl.ds(0, TM_EXPERT)], buf.at[slot], sem.at[slot]).wait()
        hb = buf[slot].astype(BF16)
        g = jnp.dot(hb, wg_ref[0], preferred_element_type=F32)
        u = jnp.dot(hb, wu_ref[0], preferred_element_type=F32)
        a = g * (1.0 / (1.0 + jnp.exp(-g))) * u
        y_ref[...] = jnp.dot(a.astype(BF16), wd_ref[0], preferred_element_type=F32)

    @pl.when(i >= nv)
    def _():
        y_ref[...] = jnp.zeros_like(y_ref)


def _experts(h, tile_expert, n_valid, sorted_tok, wg, wu, wd):
    t, d = h.shape
    rows_pad = sorted_tok.shape[0]
    n_tiles = rows_pad // TM_EXPERT
    grid_spec = pltpu.PrefetchScalarGridSpec(
        num_scalar_prefetch=3,
        grid=(n_tiles,),
        in_specs=[
            pl.BlockSpec(memory_space=pl.ANY),
            pl.BlockSpec((1, d, D_EXPERT), lambda i, te, nv, tok: (te[i], 0, 0)),
            pl.BlockSpec((1, d, D_EXPERT), lambda i, te, nv, tok: (te[i], 0, 0)),
            pl.BlockSpec((1, D_EXPERT, d), lambda i, te, nv, tok: (te[i], 0, 0)),
        ],
        out_specs=pl.BlockSpec((TM_EXPERT, d), lambda i, te, nv, tok: (i, 0)),
        scratch_shapes=[pltpu.VMEM((2, TM_EXPERT, d), F32), pltpu.SemaphoreType.DMA((2,))],
    )
    return pl.pallas_call(
        _expert_kernel,
        out_shape=jax.ShapeDtypeStruct((rows_pad, d), F32),
        grid_spec=grid_spec,
        compiler_params=_params("arbitrary"),
        name="experts",
    )(tile_expert, n_valid, sorted_tok, h, wg, wu, wd)


def _combine_kernel(pos_ref, y_hbm, x_ref, w_ref, gf_ref, lng_ref, lnb_ref, o_ref, buf, sem, *, n_tok):
    i = pl.program_id(0)
    n = pl.num_programs(0)
    slot = i % 2
    tm = x_ref.shape[0]

    def issue(tile, slot_):
        base = tile * tm

        def body(r, carry):
            for k in range(TOP_K):
                p = pos_ref[k * n_tok + base + r]
                pltpu.make_async_copy(y_hbm.at[pl.ds(p, 1)], buf.at[slot_, k, pl.ds(r, 1)],
                                      sem.at[slot_]).start()
            return carry

        lax.fori_loop(0, tm, body, 0, unroll=8)

    @pl.when(i == 0)
    def _():
        issue(0, 0)

    @pl.when(i + 1 < n)
    def _():
        issue(i + 1, 1 - slot)

    for k in range(TOP_K):
        pltpu.make_async_copy(y_hbm.at[pl.ds(0, tm)], buf.at[slot, k], sem.at[slot]).wait()
    w = w_ref[...]
    m = w[:, 0:1] * buf[slot, 0] + w[:, 1:2] * buf[slot, 1]
    z = DEEPNORM_ALPHA * x_ref[...] + (1.0 + gf_ref[0]) * m
    o_ref[...] = _layer_norm(z, lng_ref[...], lnb_ref[...])


def _combine(pos, y, x, wts_t, g_f, ln_g, ln_b, seq):
    t, d = x.shape
    tm = TM_COMBINE
    tiles_per_seq = seq // tm
    grid_spec = pltpu.PrefetchScalarGridSpec(
        num_scalar_prefetch=1,
        grid=(t // tm,),
        in_specs=[
            pl.BlockSpec(memory_space=pl.ANY),
            pl.BlockSpec((tm, d), lambda i, pos_: (i, 0)),
            pl.BlockSpec((tm, TOP_K), lambda i, pos_: (i, 0)),
            pl.BlockSpec((1, 1, d), lambda i, pos_: (i // tiles_per_seq, 0, 0)),
            pl.BlockSpec((1, d), lambda i, pos_: (0, 0)),
            pl.BlockSpec((1, d), lambda i, pos_: (0, 0)),
        ],
        out_specs=pl.BlockSpec((tm, d), lambda i, pos_: (i, 0)),
        scratch_shapes=[pltpu.VMEM((2, TOP_K, tm, d), F32), pltpu.SemaphoreType.DMA((2,))],
    )
    return pl.pallas_call(
        functools.partial(_combine_kernel, n_tok=t),
        out_shape=jax.ShapeDtypeStruct((t, d), F32),
        grid_spec=grid_spec,
        compiler_params=_params("arbitrary"),
        name="combine",
    )(pos, y, x, wts_t, g_f, ln_g, ln_b)


def kernel(x, c, ada_w, ada_b, ln_attn_g, ln_attn_b, ln_ffn_g, ln_ffn_b, fox_w_in, fox_b_f, fox_w_o,
           kv_ada_w, kv_ada_b, kv_w, diff_w_q, diff_lambda, diff_norm_g, diff_w_o, router_w, router_b,
           moe_w_gate, moe_w_up, moe_w_down):
    bsz, seq, d = x.shape
    t = bsz * seq
    rows_pad = TOP_K * t + N_EXPERTS * TM_EXPERT
    nq = seq // TQ

    mod = _ada(c, ada_w, ada_b, 1536)
    kv_mod = _ada(c, kv_ada_w[None], kv_ada_b[None], 1024)[0]

    def mod_part(l, k):
        return mod[l, :, k * d:(k + 1) * d].reshape(bsz, 1, d)

    router_wt = router_w.T
    router_bc = router_b.reshape(N_EXPERTS, 1)
    xt = x.reshape(t, d)

    k_sh = v_sh_t = None
    alibi_kb = _alibi_key_bias(seq)
    for l in range(DEPTH):
        sh_a, sc_a, g_a, sh_f, sc_f, g_f = (mod_part(l, k) for k in range(6))
        if l < N_A:
            w_in = fox_w_in[l]
            w_qt = w_in[:, :d].T.astype(BF16)
            w_k = w_in[:, d:2 * d].astype(BF16)
            w_vt = w_in[:, 2 * d:3 * d].T.astype(BF16)
            w_f = jnp.pad(w_in[:, 3 * d:], ((0, 0), (0, HEAD_BLOCK - FOX_HEADS))).astype(BF16)
            b_f = jnp.pad(fox_b_f[l], (0, HEAD_BLOCK - FOX_HEADS)).reshape(1, HEAD_BLOCK)
            k_arr, q_t, v_t, cum, cum_t = _modproj(xt, sh_a, sc_a, [w_k], [w_qt, w_vt], seq,
                                                   forget=(w_f, b_f))
            o = _attention(
                "fox", q_t, k_arr, 0, _fox_key_bias(cum, bsz, seq),
                pl.BlockSpec((1, 1, seq, HEAD_BLOCK), lambda b, h, i: (b, h, 0, 0)),
                v_t, 0, (cum_t.reshape(N_HEAD_BLOCKS, 2, t),),
                [pl.BlockSpec((1, 2, TQ), lambda b, h, i: (h, 0, b * nq + i))],
                bsz, seq)
            w_o = fox_w_o[l].astype(BF16)
        else:
            j = l - N_A
            if l == N_A:
                k_sh, v_sh_t = _modproj(xt, kv_mod[:, :d].reshape(bsz, 1, d),
                                        kv_mod[:, d:].reshape(bsz, 1, d),
                                        [kv_w[:, :d].astype(BF16)], [kv_w[:, d:].T.astype(BF16)], seq)
            (q_t,) = _modproj(xt, sh_a, sc_a, [], [diff_w_q[j].T.astype(BF16)], seq)
            lam_init = 0.8 - 0.6 * math.exp(-0.3 * l)
            o = _attention(
                "diff", q_t, k_sh, 0, alibi_kb,
                pl.BlockSpec((seq, HEAD_BLOCK), lambda b, h, i: (0, 0)),
                v_sh_t, 0,
                (diff_lambda[j][None], diff_norm_g[j].reshape(1, HEAD_BLOCK)),
                [pl.BlockSpec((1, 4, HALF), lambda b, h, i: (0, 0, 0)),
                 pl.BlockSpec((1, HEAD_BLOCK), lambda b, h, i: (0, 0))],
                bsz, seq, lam_init=lam_init)
            w_o = diff_w_o[j].astype(BF16)

        xt, h, eidx, wts = _post_attn(o, xt, w_o, g_a, ln_attn_g[l].reshape(1, d),
                                      ln_attn_b[l].reshape(1, d), sh_f, sc_f, router_wt, router_bc, seq)
        pos, sorted_tok, tile_expert, n_valid = _routing_tables(eidx, rows_pad)
        y = _experts(h, tile_expert, n_valid, sorted_tok, moe_w_gate[l].astype(BF16),
                     moe_w_up[l].astype(BF16), moe_w_down[l].astype(BF16))
        xt = _combine(pos, y, xt, wts.T, g_f, ln_ffn_g[l].reshape(1, d), ln_ffn_b[l].reshape(1, d), seq)
    return xt.reshape(bsz, seq, d)
```

```python
import functools
import math

import jax
import jax.numpy as jnp
import numpy as np
from jax import lax
from jax.experimental import pallas as pl
from jax.experimental.pallas import tpu as pltpu

F32 = jnp.float32
BF16 = jnp.bfloat16
HIGHEST = lax.Precision.HIGHEST

D_MODEL = 1024
DEPTH = 4
N_A = DEPTH // 2
HEAD_BLOCK = 128
N_HEAD_BLOCKS = D_MODEL // HEAD_BLOCK
HALF = HEAD_BLOCK // 2
FOX_HEADS = 16
DIFF_HEADS = 8
N_EXPERTS = 16
N_GROUPS = 4
EXPERTS_PER_GROUP = N_EXPERTS // N_GROUPS
TOP_K = 2
D_EXPERT = 512
DEEPNORM_ALPHA = (2 * DEPTH) ** 0.25
LN_EPS = 1e-5
NEG_INF = -1e30
LOG2E = math.log2(math.e)

VMEM_LIMIT = 56 * 1024 * 1024

TM_PROJ = 512
TN_PROJ = 256
TQ = 512
TK = 512
BIAS_ROWS = 16
ONES_ROWS = 16
TM_EXPERT = 512
TM_COMBINE = 256

NT_DIMS = (((1,), (1,)), ((), ()))


def _params(*sem):
    return pltpu.CompilerParams(dimension_semantics=sem, vmem_limit_bytes=VMEM_LIMIT)


def _split3(v):
    hi = v.astype(BF16).astype(F32)
    r = v - hi
    mid = r.astype(BF16).astype(F32)
    lo = (r - mid).astype(BF16).astype(F32)
    return hi, mid, lo


def _ada_kernel(c_ref, w_ref, b_ref, o_ref):
    c = c_ref[...]
    sc = c * (1.0 / (1.0 + jnp.exp(-c)))
    o_ref[0] = jnp.dot(sc, w_ref[0], precision=HIGHEST, preferred_element_type=F32) + b_ref[0]


def _ada(c, w, b, tn):
    n_layers, d, n = w.shape
    bsz = c.shape[0]
    return pl.pallas_call(
        _ada_kernel,
        out_shape=jax.ShapeDtypeStruct((n_layers, bsz, n), F32),
        grid=(n_layers, n // tn),
        in_specs=[
            pl.BlockSpec((bsz, d), lambda l, j: (0, 0)),
            pl.BlockSpec((1, d, tn), lambda l, j: (l, 0, j)),
            pl.BlockSpec((1, 1, tn), lambda l, j: (l, 0, j)),
        ],
        out_specs=pl.BlockSpec((1, bsz, tn), lambda l, j: (l, 0, j)),
        compiler_params=_params("parallel", "parallel"),
        name="ada_params",
    )(c, w, b.reshape(n_layers, 1, n))


def _modproj_kernel(*refs, n_std, n_t, with_forget, tiles_per_seq):
    x_ref, sh_ref, sc_ref = refs[:3]
    pos = 3
    wstd = refs[pos:pos + n_std]; pos += n_std
    wt = refs[pos:pos + n_t]; pos += n_t
    if with_forget:
        wf_ref, bf_ref, sel_ref, ones_ref = refs[pos:pos + 4]; pos += 4
    ostd = refs[pos:pos + n_std]; pos += n_std
    ot = refs[pos:pos + n_t]; pos += n_t
    if with_forget:
        kb_ref, cumt_ref, carry_ref = refs[pos:pos + 3]

    hb = (x_ref[...] * (1.0 + sc_ref[0]) + sh_ref[0]).astype(BF16)
    for w_ref, o_ref in zip(wstd, ostd):
        for j in range(w_ref.shape[1] // TN_PROJ):
            cols = slice(j * TN_PROJ, (j + 1) * TN_PROJ)
            o_ref[:, cols] = jnp.dot(hb, w_ref[:, cols], preferred_element_type=F32).astype(o_ref.dtype)
    for w_ref, o_ref in zip(wt, ot):
        for j in range(w_ref.shape[0] // TN_PROJ):
            rows = slice(j * TN_PROJ, (j + 1) * TN_PROJ)
            o_ref[rows, :] = lax.dot_general(w_ref[rows, :], hb, NT_DIMS,
                                             preferred_element_type=F32).astype(o_ref.dtype)

    if with_forget:
        @pl.when(pl.program_id(0) % tiles_per_seq == 0)
        def _():
            carry_ref[...] = jnp.zeros_like(carry_ref)

        z = jnp.dot(hb, wf_ref[...], preferred_element_type=F32) + bf_ref[...]
        logf = jnp.minimum(z, 0.0) - jnp.log(1.0 + jnp.exp(-jnp.abs(z)))
        tm = z.shape[0]
        lower = (lax.broadcasted_iota(jnp.int32, (tm, tm), 0)
                 >= lax.broadcasted_iota(jnp.int32, (tm, tm), 1)).astype(F32)
        cum = jnp.dot(lower, logf, precision=HIGHEST, preferred_element_type=F32) + carry_ref[...]
        cumt_ref[...] = cum.T[:FOX_HEADS, :]
        carry_ref[...] = cum[tm - 1:tm, :]
        terms = jnp.concatenate([v.astype(BF16) for v in _split3((-LOG2E) * cum)], axis=1)
        for j in range(sel_ref.shape[1] // TN_PROJ):
            cols = slice(j * TN_PROJ, (j + 1) * TN_PROJ)
            kb_ref[:, cols] = (jnp.dot(terms, sel_ref[:, cols], preferred_element_type=F32)
                               + ones_ref[:, cols]).astype(kb_ref.dtype)


def _fox_bias_selection():
    sel = np.zeros((3 * HEAD_BLOCK, D_MODEL), np.float32)
    ones = np.zeros((1, D_MODEL), np.float32)
    for hp in range(N_HEAD_BLOCKS):
        for a in range(2):
            for term in range(3):
                sel[term * HEAD_BLOCK + 2 * hp + a, hp * HEAD_BLOCK + 6 * a + term] = 1.0
            ones[0, hp * HEAD_BLOCK + 6 * a + 3:hp * HEAD_BLOCK + 6 * a + 6] = 1.0
    return jnp.asarray(sel, BF16), jnp.asarray(ones, F32)


def _modproj(x, shift, scale, w_std, w_t, seq, forget=None):
    t, d = x.shape
    tm = TM_PROJ
    tiles_per_seq = seq // tm
    const = lambda i: (0, 0)
    mod_spec = pl.BlockSpec((1, 1, d), lambda i: (i // tiles_per_seq, 0, 0))
    in_specs = [pl.BlockSpec((tm, d), lambda i: (i, 0)), mod_spec, mod_spec]
    in_specs += [pl.BlockSpec(w.shape, const) for w in w_std]
    in_specs += [pl.BlockSpec(w.shape, const) for w in w_t]
    out_shape = [jax.ShapeDtypeStruct((t, w.shape[1]), BF16) for w in w_std]
    out_shape += [jax.ShapeDtypeStruct((w.shape[0], t), BF16) for w in w_t]
    out_specs = [pl.BlockSpec((tm, w.shape[1]), lambda i: (i, 0)) for w in w_std]
    out_specs += [pl.BlockSpec((w.shape[0], tm), lambda i: (0, i)) for w in w_t]
    args = [x, shift, scale, *w_std, *w_t]
    scratch = []
    if forget is not None:
        wf, bf = forget
        sel, ones = _fox_bias_selection()
        in_specs += [pl.BlockSpec(a.shape, const) for a in (wf, bf, sel, ones)]
        args += [wf, bf, sel, ones]
        out_shape += [jax.ShapeDtypeStruct((t, d), BF16), jax.ShapeDtypeStruct((FOX_HEADS, t), F32)]
        out_specs += [pl.BlockSpec((tm, d), lambda i: (i, 0)),
                      pl.BlockSpec((FOX_HEADS, tm), lambda i: (0, i))]
        scratch = [pltpu.VMEM((1, HEAD_BLOCK), F32)]
    return pl.pallas_call(
        functools.partial(_modproj_kernel, n_std=len(w_std), n_t=len(w_t),
                          with_forget=forget is not None, tiles_per_seq=tiles_per_seq),
        out_shape=tuple(out_shape),
        grid=(t // tm,),
        in_specs=in_specs,
        out_specs=tuple(out_specs),
        scratch_shapes=scratch,
        compiler_params=_params("arbitrary"),
        name="modproj_fox" if forget is not None else "modproj",
    )(*args)


def _bias_rows(entries, tq):
    rid = lax.broadcasted_iota(jnp.int32, (BIAS_ROWS, tq), 0)
    out = jnp.zeros((BIAS_ROWS, tq), F32)
    for r, v in entries.items():
        out = jnp.where(rid == r, v, out)
    return out


def _attn_kernel(*refs, mode, lam_init, n_heads):
    if mode == "fox":
        q_ref, k_ref, kb_ref, v_ref, cumt_ref, o_ref, m_ref, acc_ref = refs
    else:
        q_ref, k_ref, kb_ref, v_ref, lam_ref, g_ref, o_ref, m_ref, acc_ref = refs
    hp = pl.program_id(1)
    i = pl.program_id(2)
    tq = q_ref.shape[1]

    qs = q_ref[...] * jnp.asarray(HALF ** -0.5 * LOG2E, BF16)
    zeros_half = jnp.zeros((HALF, tq), BF16)
    q_tops = (jnp.concatenate([qs[:HALF], zeros_half], axis=0),
              jnp.concatenate([zeros_half, qs[HALF:]], axis=0))
    pad = jnp.zeros((HEAD_BLOCK - BIAS_ROWS, tq), BF16)
    if mode == "fox":
        q_bias = []
        for a in range(2):
            c = _split3(cumt_ref[0, a:a + 1, :] * LOG2E)
            base = 6 * a
            rows = {base: 1.0, base + 1: 1.0, base + 2: 1.0,
                    base + 3: c[0], base + 4: c[1], base + 5: c[2]}
            q_bias.append(_bias_rows(rows, tq).astype(BF16))
    else:
        slope = jnp.exp2(jnp.full((1, 1), -8.0, F32) * (hp + 1).astype(F32) / n_heads)
        l3 = _split3(slope * LOG2E)
        qpos = (i * tq + lax.broadcasted_iota(jnp.int32, (1, tq), 1)).astype(F32)
        t3 = _split3(-(slope * LOG2E) * qpos)
        rows = {0: l3[0], 1: l3[0], 2: l3[1], 3: l3[1], 4: l3[2], 5: l3[2],
                6: t3[0], 7: t3[1], 8: t3[2]}
        q_bias = [_bias_rows(rows, tq).astype(BF16)] * 2
    q_aug = tuple(jnp.concatenate([q_tops[a], q_bias[a], pad], axis=0) for a in range(2))

    m_ref[...] = jnp.full_like(m_ref, NEG_INF)
    acc_ref[...] = jnp.zeros_like(acc_ref)
    ones_rows = jnp.ones((ONES_ROWS, TK), BF16)

    def key_slice(j):
        return pl.ds(pl.multiple_of(j * TK, TK), TK)

    def scores(j, a):
        ks = key_slice(j)
        k_aug = jnp.concatenate([k_ref[ks, :], kb_ref[ks, :]], axis=1)
        return jnp.dot(k_aug, q_aug[a], preferred_element_type=F32)

    v_rows = [slice(a * HALF, (a + 1) * HALF) if mode == "fox" else slice(0, HEAD_BLOCK)
              for a in range(2)]
    n_v = v_rows[0].stop - v_rows[0].start

    def accumulate(j, a, s, masked):
        v_aug = jnp.concatenate([v_ref[v_rows[a], key_slice(j)], ones_rows], axis=0)
        if masked:
            keep = ((j * TK + lax.broadcasted_iota(jnp.int32, (TK, tq), 0))
                    <= (i * tq + lax.broadcasted_iota(jnp.int32, (TK, tq), 1)))
            s = jnp.where(keep, s, NEG_INF)
        m_old = m_ref[a]
        m_new = jnp.maximum(m_old, jnp.max(s, axis=0, keepdims=True))
        alpha = jnp.exp2(m_old - m_new)
        p = jnp.exp2(s - m_new).astype(BF16)
        acc_ref[a] = alpha * acc_ref[a] + jnp.dot(v_aug, p, preferred_element_type=F32)
        m_ref[a] = m_new

    def process(blocks):
        work = [(j, a, masked) for (j, masked) in blocks for a in range(2)]
        pending = scores(work[0][0], work[0][1])
        for n, (j, a, masked) in enumerate(work):
            s = pending
            if n + 1 < len(work):
                pending = scores(work[n + 1][0], work[n + 1][1])
            accumulate(j, a, s, masked)

    n_full = (i * tq) // TK

    def pair_body(p, carry):
        process([(2 * p, False), (2 * p + 1, False)])
        return carry

    lax.fori_loop(0, n_full // 2, pair_body, 0)

    @pl.when(n_full % 2 == 1)
    def _():
        process([(n_full - 1, False), (n_full, True)])

    @pl.when(n_full % 2 == 0)
    def _():
        process([(n_full, True)])

    inv = [1.0 / acc_ref[a, n_v:n_v + 1, :] for a in range(2)]
    if mode == "fox":
        o_t = jnp.concatenate([acc_ref[a, :n_v, :] * inv[a] for a in range(2)], axis=0)
        o_ref[...] = o_t.T.astype(o_ref.dtype)
    else:
        lp = lam_ref[0]
        lam = (jnp.exp(jnp.sum(lp[0:1] * lp[1:2], axis=-1, keepdims=True))
               - jnp.exp(jnp.sum(lp[2:3] * lp[3:4], axis=-1, keepdims=True)) + lam_init)
        o_t = acc_ref[0, :HEAD_BLOCK, :] * inv[0] - lam * (acc_ref[1, :HEAD_BLOCK, :] * inv[1])
        o = o_t.T
        o = o * lax.rsqrt(jnp.mean(o * o, axis=-1, keepdims=True) + LN_EPS)
        o_ref[...] = (o * g_ref[...] * (1.0 - lam_init)).astype(o_ref.dtype)


def _attention(mode, q_t, k_arr, k_off, kb, kb_spec, v_t, v_off, extras, extra_specs, batch, seq,
               lam_init=0.0, n_heads=DIFF_HEADS):
    assert TQ == TK, "one masked diagonal block per query tile"
    nq = seq // TQ
    t = batch * seq
    acc_rows = (HALF if mode == "fox" else HEAD_BLOCK) + ONES_ROWS
    in_specs = [
        pl.BlockSpec((HEAD_BLOCK, TQ), lambda b, h, i: (h, b * nq + i)),
        pl.BlockSpec((seq, HEAD_BLOCK), lambda b, h, i: (b, k_off + h)),
        kb_spec,
        pl.BlockSpec((HEAD_BLOCK, seq), lambda b, h, i: (v_off + h, b)),
    ] + extra_specs
    return pl.pallas_call(
        functools.partial(_attn_kernel, mode=mode, lam_init=lam_init, n_heads=n_heads),
        out_shape=jax.ShapeDtypeStruct((t, D_MODEL), BF16),
        grid=(batch, N_HEAD_BLOCKS, nq),
        in_specs=in_specs,
        out_specs=pl.BlockSpec((TQ, HEAD_BLOCK), lambda b, h, i: (b * nq + i, h)),
        scratch_shapes=[pltpu.VMEM((2, 1, TQ), F32),
                        pltpu.VMEM((2, acc_rows, TQ), F32)],
        compiler_params=_params("parallel", "parallel", "arbitrary"),
        name="attn_" + mode,
    )(q_t, k_arr, kb, v_t, *extras)


def _bias_lanes(cols):
    lanes = jnp.stack(cols, axis=-1)
    pad = [(0, 0)] * (lanes.ndim - 1) + [(0, HEAD_BLOCK - lanes.shape[-1])]
    return jnp.pad(lanes, pad).astype(BF16)


def _alibi_key_bias(seq):
    kpos = jnp.arange(seq, dtype=jnp.int32)
    hi = ((kpos // 64) * 64).astype(F32)
    lo = (kpos % 64).astype(F32)
    one = jnp.ones((seq,), F32)
    return _bias_lanes([hi, lo, hi, lo, hi, lo, one, one, one])


def _layer_norm(z, g, b):
    mu = jnp.mean(z, axis=-1, keepdims=True)
    zc = z - mu
    var = jnp.mean(zc * zc, axis=-1, keepdims=True)
    return zc * lax.rsqrt(var + LN_EPS) * g + b


def _argmax_first(vals):
    best, idx = vals[0], jnp.zeros(vals[0].shape, jnp.int32)
    for j in range(1, len(vals)):
        better = vals[j] > best
        idx = jnp.where(better, j, idx)
        best = jnp.where(better, vals[j], best)
    return idx


def _select(vals, idx):
    out = vals[0]
    for j in range(1, len(vals)):
        out = jnp.where(idx == j, vals[j], out)
    return out


def _post_attn_kernel(o_ref, x_ref, wo_ref, ga_ref, lng_ref, lnb_ref, shf_ref, scf_ref,
                      rwt_ref, rb_ref, xo_ref, h_ref, eidx_ref, wts_ref):
    y = jnp.dot(o_ref[...], wo_ref[...], preferred_element_type=F32)
    z = DEEPNORM_ALPHA * x_ref[...] + (1.0 + ga_ref[0]) * y
    xn = _layer_norm(z, lng_ref[...], lnb_ref[...])
    xo_ref[...] = xn
    h = xn * (1.0 + scf_ref[0]) + shf_ref[0]
    h_ref[...] = h

    logits = lax.dot_general(rwt_ref[...], h, NT_DIMS, precision=HIGHEST,
                             preferred_element_type=F32)
    rows = [logits[e:e + 1, :] for e in range(N_EXPERTS)]
    mx = functools.reduce(jnp.maximum, rows)
    ex = [jnp.exp(r - mx) for r in rows]
    den = functools.reduce(lambda a, b: a + b, ex)
    probs = [e_ / den for e_ in ex]
    sel = [probs[e] + rb_ref[e:e + 1, :] for e in range(N_EXPERTS)]

    grp_scores = []
    for g in range(N_GROUPS):
        v = sel[g * EXPERTS_PER_GROUP:(g + 1) * EXPERTS_PER_GROUP]
        pair_sums = [v[a] + v[b] for a in range(EXPERTS_PER_GROUP) for b in range(a + 1, EXPERTS_PER_GROUP)]
        grp_scores.append(functools.reduce(jnp.maximum, pair_sums))
    g_idx = _argmax_first(grp_scores)

    in_sel = [_select([sel[g * EXPERTS_PER_GROUP + j] for g in range(N_GROUPS)], g_idx)
              for j in range(EXPERTS_PER_GROUP)]
    in_prob = [_select([probs[g * EXPERTS_PER_GROUP + j] for g in range(N_GROUPS)], g_idx)
               for j in range(EXPERTS_PER_GROUP)]
    i1 = _argmax_first(in_sel)
    b2 = jnp.full(in_sel[0].shape, -jnp.inf, F32)
    i2 = jnp.zeros(in_sel[0].shape, jnp.int32)
    for j in range(EXPERTS_PER_GROUP):
        cand = (i1 != j) & (in_sel[j] > b2)
        i2 = jnp.where(cand, j, i2)
        b2 = jnp.where(cand, in_sel[j], b2)
    p1 = _select(in_prob, i1)
    p2 = _select(in_prob, i2)
    tot = p1 + p2
    eidx_ref[0:1, :] = g_idx * EXPERTS_PER_GROUP + i1
    eidx_ref[1:2, :] = g_idx * EXPERTS_PER_GROUP + i2
    wts_ref[0:1, :] = p1 / tot
    wts_ref[1:2, :] = p2 / tot


def _post_attn(o, x, wo, g_a, ln_g, ln_b, sh_f, sc_f, router_wt, router_b, seq):
    t, d = x.shape
    tm = TM_PROJ
    tiles_per_seq = seq // tm
    mod_spec = pl.BlockSpec((1, 1, d), lambda i: (i // tiles_per_seq, 0, 0))
    vec_spec = pl.BlockSpec((1, d), lambda i: (0, 0))
    tok_spec = pl.BlockSpec((tm, d), lambda i: (i, 0))
    route_spec = pl.BlockSpec((TOP_K, tm), lambda i: (0, i))
    return pl.pallas_call(
        _post_attn_kernel,
        out_shape=(jax.ShapeDtypeStruct((t, d), F32), jax.ShapeDtypeStruct((t, d), F32),
                   jax.ShapeDtypeStruct((TOP_K, t), jnp.int32), jax.ShapeDtypeStruct((TOP_K, t), F32)),
        grid=(t // tm,),
        in_specs=[
            tok_spec, tok_spec,
            pl.BlockSpec((d, d), lambda i: (0, 0)),
            mod_spec, vec_spec, vec_spec, mod_spec, mod_spec,
            pl.BlockSpec((N_EXPERTS, d), lambda i: (0, 0)),
            pl.BlockSpec((N_EXPERTS, 1), lambda i: (0, 0)),
        ],
        out_specs=(tok_spec, tok_spec, route_spec, route_spec),
        compiler_params=_params("parallel"),
        name="post_attn",
    )(o, x, wo, g_a, ln_g, ln_b, sh_f, sc_f, router_wt, router_b)


def _routing_tables(eidx, rows_pad):
    t = eidx.shape[1]
    e_flat = eidx.reshape(-1)
    onehot = (e_flat[:, None] == jnp.arange(N_EXPERTS, dtype=jnp.int32)[None, :]).astype(jnp.int32)
    csum = jnp.cumsum(onehot, axis=0)
    rank = jnp.sum(csum * onehot, axis=1) - 1
    counts = csum[-1]
    padded = ((counts + TM_EXPERT - 1) // TM_EXPERT) * TM_EXPERT
    ends = jnp.cumsum(padded)
    starts = ends - padded
    pos = jnp.sum(onehot * starts[None, :], axis=1) + rank
    tok = jnp.tile(jnp.arange(t, dtype=jnp.int32), TOP_K)
    sorted_tok = (jnp.arange(rows_pad, dtype=jnp.int32) % t).at[pos].set(tok)
    n_tiles = rows_pad // TM_EXPERT
    tile_start = jnp.arange(n_tiles, dtype=jnp.int32) * TM_EXPERT
    tile_expert = jnp.minimum(jnp.sum((tile_start[:, None] >= ends[None, :]).astype(jnp.int32), axis=1),
                              N_EXPERTS - 1).astype(jnp.int32)
    n_valid = (ends[-1] // TM_EXPERT).astype(jnp.int32).reshape(1)
    return pos.astype(jnp.int32), sorted_tok, tile_expert, n_valid


def _expert_kernel(te_ref, nv_ref, tok_ref, h_hbm, wg_ref, wu_ref, wd_ref, y_ref, buf, sem):
    i = pl.program_id(0)
    nv = nv_ref[0]
    slot = i % 2

    def issue(tile, slot_):
        base = tile * TM_EXPERT

        def body(r, carry):
            tok = tok_ref[base + r]
            pltpu.make_async_copy(h_hbm.at[pl.ds(tok, 1)], buf.at[slot_, pl.ds(r, 1)],
                                  sem.at[slot_]).start()
            return carry

        lax.fori_loop(0, TM_EXPERT, body, 0, unroll=8)

    @pl.when((i == 0) & (nv > 0))
    def _():
        issue(0, 0)

    @pl.when(i + 1 < nv)
    def _():
        issue(i + 1, 1 - slot)

    @pl.when(i < nv)
    def _():
        pltpu.make_async_copy(h_hbm.at[pl.ds(0, TM_EXPERT)], buf.at[slot], sem.at[slot]).wait()
        hb = buf[slot].astype(BF16)
        g = jnp.dot(hb, wg_ref[0], preferred_element_type=F32)
        u = jnp.dot(hb, wu_ref[0], preferred_element_type=F32)
        a = g * (1.0 / (1.0 + jnp.exp(-g))) * u
        y_ref[...] = jnp.dot(a.astype(BF16), wd_ref[0], preferred_element_type=F32)

    @pl.when(i >= nv)
    def _():
        y_ref[...] = jnp.zeros_like(y_ref)


def _experts(h, tile_expert, n_valid, sorted_tok, wg, wu, wd):
    t, d = h.shape
    rows_pad = sorted_tok.shape[0]
    n_tiles = rows_pad // TM_EXPERT
    grid_spec = pltpu.PrefetchScalarGridSpec(
        num_scalar_prefetch=3,
        grid=(n_tiles,),
        in_specs=[
            pl.BlockSpec(memory_space=pl.ANY),
            pl.BlockSpec((1, d, D_EXPERT), lambda i, te, nv, tok: (te[i], 0, 0)),
            pl.BlockSpec((1, d, D_EXPERT), lambda i, te, nv, tok: (te[i], 0, 0)),
            pl.BlockSpec((1, D_EXPERT, d), lambda i, te, nv, tok: (te[i], 0, 0)),
        ],
        out_specs=pl.BlockSpec((TM_EXPERT, d), lambda i, te, nv, tok: (i, 0)),
        scratch_shapes=[pltpu.VMEM((2, TM_EXPERT, d), F32), pltpu.SemaphoreType.DMA((2,))],
    )
    return pl.pallas_call(
        _expert_kernel,
        out_shape=jax.ShapeDtypeStruct((rows_pad, d), F32),
        grid_spec=grid_spec,
        compiler_params=_params("arbitrary"),
        name="experts",
    )(tile_expert, n_valid, sorted_tok, h, wg, wu, wd)


def _combine_kernel(pos_ref, y_hbm, x_ref, w_ref, gf_ref, lng_ref, lnb_ref, o_ref, buf, sem, *, n_tok):
    i = pl.program_id(0)
    n = pl.num_programs(0)
    slot = i % 2
    tm = x_ref.shape[0]

    def issue(tile, slot_):
        base = tile * tm

        def body(r, carry):
            for k in range(TOP_K):
                p = pos_ref[k * n_tok + base + r]
                pltpu.make_async_copy(y_hbm.at[pl.ds(p, 1)], buf.at[slot_, k, pl.ds(r, 1)],
                                      sem.at[slot_]).start()
            return carry

        lax.fori_loop(0, tm, body, 0, unroll=8)

    @pl.when(i == 0)
    def _():
        issue(0, 0)

    @pl.when(i + 1 < n)
    def _():
        issue(i + 1, 1 - slot)

    for k in range(TOP_K):
        pltpu.make_async_copy(y_hbm.at[pl.ds(0, tm)], buf.at[slot, k], sem.at[slot]).wait()
    w = w_ref[...]
    m = w[:, 0:1] * buf[slot, 0] + w[:, 1:2] * buf[slot, 1]
    z = DEEPNORM_ALPHA * x_ref[...] + (1.0 + gf_ref[0]) * m
    o_ref[...] = _layer_norm(z, lng_ref[...], lnb_ref[...])


def _combine(pos, y, x, wts_t, g_f, ln_g, ln_b, seq):
    t, d = x.shape
    tm = TM_COMBINE
    tiles_per_seq = seq // tm
    grid_spec = pltpu.PrefetchScalarGridSpec(
        num_scalar_prefetch=1,
        grid=(t // tm,),
        in_specs=[
            pl.BlockSpec(memory_space=pl.ANY),
            pl.BlockSpec((tm, d), lambda i, pos_: (i, 0)),
            pl.BlockSpec((tm, TOP_K), lambda i, pos_: (i, 0)),
            pl.BlockSpec((1, 1, d), lambda i, pos_: (i // tiles_per_seq, 0, 0)),
            pl.BlockSpec((1, d), lambda i, pos_: (0, 0)),
            pl.BlockSpec((1, d), lambda i, pos_: (0, 0)),
        ],
        out_specs=pl.BlockSpec((tm, d), lambda i, pos_: (i, 0)),
        scratch_shapes=[pltpu.VMEM((2, TOP_K, tm, d), F32), pltpu.SemaphoreType.DMA((2,))],
    )
    return pl.pallas_call(
        functools.partial(_combine_kernel, n_tok=t),
        out_shape=jax.ShapeDtypeStruct((t, d), F32),
        grid_spec=grid_spec,
        compiler_params=_params("arbitrary"),
        name="combine",
    )(pos, y, x, wts_t, g_f, ln_g, ln_b)


def kernel(x, c, ada_w, ada_b, ln_attn_g, ln_attn_b, ln_ffn_g, ln_ffn_b, fox_w_in, fox_b_f, fox_w_o,
           kv_ada_w, kv_ada_b, kv_w, diff_w_q, diff_lambda, diff_norm_g, diff_w_o, router_w, router_b,
           moe_w_gate, moe_w_up, moe_w_down):
    bsz, seq, d = x.shape
    t = bsz * seq
    rows_pad = TOP_K * t + N_EXPERTS * TM_EXPERT
    nq = seq // TQ

    mod = _ada(c, ada_w, ada_b, 1536)
    kv_mod = _ada(c, kv_ada_w[None], kv_ada_b[None], 1024)[0]

    def mod_part(l, k):
        return mod[l, :, k * d:(k + 1) * d].reshape(bsz, 1, d)

    router_wt = router_w.T
    router_bc = router_b.reshape(N_EXPERTS, 1)
    xt = x.reshape(t, d)

    k_sh = v_sh_t = None
    alibi_kb = _alibi_key_bias(seq)
    for l in range(DEPTH):
        sh_a, sc_a, g_a, sh_f, sc_f, g_f = (mod_part(l, k) for k in range(6))
        if l < N_A:
            w_in = fox_w_in[l]
            w_qt = w_in[:, :d].T.astype(BF16)
            w_k = w_in[:, d:2 * d].astype(BF16)
            w_vt = w_in[:, 2 * d:3 * d].T.astype(BF16)
            w_f = jnp.pad(w_in[:, 3 * d:], ((0, 0), (0, HEAD_BLOCK - FOX_HEADS))).astype(BF16)
            b_f = jnp.pad(fox_b_f[l], (0, HEAD_BLOCK - FOX_HEADS)).reshape(1, HEAD_BLOCK)
            k_arr, q_t, v_t, key_bias, cum_t = _modproj(xt, sh_a, sc_a, [w_k], [w_qt, w_vt], seq,
                                                        forget=(w_f, b_f))
            o = _attention(
                "fox", q_t, k_arr, 0, key_bias,
                pl.BlockSpec((seq, HEAD_BLOCK), lambda b, h, i: (b, h)),
                v_t, 0, (cum_t.reshape(N_HEAD_BLOCKS, 2, t),),
                [pl.BlockSpec((1, 2, TQ), lambda b, h, i: (h, 0, b * nq + i))],
                bsz, seq)
            w_o = fox_w_o[l].astype(BF16)
        else:
            j = l - N_A
            if l == N_A:
                k_sh, v_sh_t = _modproj(xt, kv_mod[:, :d].reshape(bsz, 1, d),
                                        kv_mod[:, d:].reshape(bsz, 1, d),
                                        [kv_w[:, :d].astype(BF16)], [kv_w[:, d:].T.astype(BF16)], seq)
            (q_t,) = _modproj(xt, sh_a, sc_a, [], [diff_w_q[j].T.astype(BF16)], seq)
            lam_init = 0.8 - 0.6 * math.exp(-0.3 * l)
            o = _attention(
                "diff", q_t, k_sh, 0, alibi_kb,
                pl.BlockSpec((seq, HEAD_BLOCK), lambda b, h, i: (0, 0)),
                v_sh_t, 0,
                (diff_lambda[j][None], diff_norm_g[j].reshape(1, HEAD_BLOCK)),
                [pl.BlockSpec((1, 4, HALF), lambda b, h, i: (0, 0, 0)),
                 pl.BlockSpec((1, HEAD_BLOCK), lambda b, h, i: (0, 0))],
                bsz, seq, lam_init=lam_init)
            w_o = diff_w_o[j].astype(BF16)

        xt, h, eidx, wts = _post_attn(o, xt, w_o, g_a, ln_attn_g[l].reshape(1, d),
                                      ln_attn_b[l].reshape(1, d), sh_f, sc_f, router_wt, router_bc, seq)
        pos, sorted_tok, tile_expert, n_valid = _routing_tables(eidx, rows_pad)
        y = _experts(h, tile_expert, n_valid, sorted_tok, moe_w_gate[l].astype(BF16),
                     moe_w_up[l].astype(BF16), moe_w_down[l].astype(BF16))
        xt = _combine(pos, y, xt, wts.T, g_f, ln_ffn_g[l].reshape(1, d), ln_ffn_b[l].reshape(1, d), seq)
    return xt.reshape(bsz, seq, d)
```

```python
import functools
import math

import jax
import jax.numpy as jnp
import numpy as np
from jax import lax
from jax.experimental import pallas as pl
from jax.experimental.pallas import tpu as pltpu

F32 = jnp.float32
BF16 = jnp.bfloat16
HIGHEST = lax.Precision.HIGHEST

D_MODEL = 1024
DEPTH = 4
N_A = DEPTH // 2
HEAD_BLOCK = 128
N_HEAD_BLOCKS = D_MODEL // HEAD_BLOCK
HALF = HEAD_BLOCK // 2
FOX_HEADS = 16
DIFF_HEADS = 8
N_EXPERTS = 16
N_GROUPS = 4
EXPERTS_PER_GROUP = N_EXPERTS // N_GROUPS
TOP_K = 2
D_EXPERT = 512
DEEPNORM_ALPHA = (2 * DEPTH) ** 0.25
LN_EPS = 1e-5
NEG_INF = -1e30
LOG2E = math.log2(math.e)

VMEM_LIMIT = 56 * 1024 * 1024

TM_PROJ = 512
TN_PROJ = 256
TQ = 512
TK = 512
QUERY_CHUNK = 512
LOOKAHEAD = 3
BIAS_ROWS = 16
ONES_ROWS = 16
TM_EXPERT = 512
TM_COMBINE = 256

NT_DIMS = (((1,), (1,)), ((), ()))


def _params(*sem, flags=None):
    return pltpu.CompilerParams(dimension_semantics=sem, vmem_limit_bytes=VMEM_LIMIT, flags=flags)


def _split3(v):
    hi = v.astype(BF16).astype(F32)
    r = v - hi
    mid = r.astype(BF16).astype(F32)
    lo = (r - mid).astype(BF16).astype(F32)
    return hi, mid, lo


def _ada_kernel(c_ref, w_ref, b_ref, o_ref):
    c = c_ref[...]
    sc = c * (1.0 / (1.0 + jnp.exp(-c)))
    o_ref[0] = jnp.dot(sc, w_ref[0], precision=HIGHEST, preferred_element_type=F32) + b_ref[0]


def _ada(c, w, b, tn):
    n_layers, d, n = w.shape
    bsz = c.shape[0]
    return pl.pallas_call(
        _ada_kernel,
        out_shape=jax.ShapeDtypeStruct((n_layers, bsz, n), F32),
        grid=(n_layers, n // tn),
        in_specs=[
            pl.BlockSpec((bsz, d), lambda l, j: (0, 0)),
            pl.BlockSpec((1, d, tn), lambda l, j: (l, 0, j)),
            pl.BlockSpec((1, 1, tn), lambda l, j: (l, 0, j)),
        ],
        out_specs=pl.BlockSpec((1, bsz, tn), lambda l, j: (l, 0, j)),
        compiler_params=_params("parallel", "parallel"),
        name="ada_params",
    )(c, w, b.reshape(n_layers, 1, n))


def _modproj_kernel(*refs, n_std, n_t, with_forget, tiles_per_seq):
    x_ref, sh_ref, sc_ref = refs[:3]
    pos = 3
    wstd = refs[pos:pos + n_std]; pos += n_std
    wt = refs[pos:pos + n_t]; pos += n_t
    if with_forget:
        wf_ref, bf_ref, sel_ref, ones_ref = refs[pos:pos + 4]; pos += 4
    ostd = refs[pos:pos + n_std]; pos += n_std
    ot = refs[pos:pos + n_t]; pos += n_t
    if with_forget:
        kb_ref, cumt_ref, carry_ref = refs[pos:pos + 3]

    hb = (x_ref[...] * (1.0 + sc_ref[0]) + sh_ref[0]).astype(BF16)
    for w_ref, o_ref in zip(wstd, ostd):
        for j in range(w_ref.shape[1] // TN_PROJ):
            cols = slice(j * TN_PROJ, (j + 1) * TN_PROJ)
            o_ref[:, cols] = jnp.dot(hb, w_ref[:, cols], preferred_element_type=F32).astype(o_ref.dtype)
    for w_ref, o_ref in zip(wt, ot):
        for j in range(w_ref.shape[0] // TN_PROJ):
            rows = slice(j * TN_PROJ, (j + 1) * TN_PROJ)
            o_ref[rows, :] = lax.dot_general(w_ref[rows, :], hb, NT_DIMS,
                                             preferred_element_type=F32).astype(o_ref.dtype)

    if with_forget:
        @pl.when(pl.program_id(0) % tiles_per_seq == 0)
        def _():
            carry_ref[...] = jnp.zeros_like(carry_ref)

        z = jnp.dot(hb, wf_ref[...], preferred_element_type=F32) + bf_ref[...]
        logf = jnp.minimum(z, 0.0) - jnp.log(1.0 + jnp.exp(-jnp.abs(z)))
        tm = z.shape[0]
        lower = (lax.broadcasted_iota(jnp.int32, (tm, tm), 0)
                 >= lax.broadcasted_iota(jnp.int32, (tm, tm), 1)).astype(F32)
        cum = jnp.dot(lower, logf, precision=HIGHEST, preferred_element_type=F32) + carry_ref[...]
        cumt_ref[...] = cum.T[:FOX_HEADS, :]
        carry_ref[...] = cum[tm - 1:tm, :]
        terms = jnp.concatenate([v.astype(BF16) for v in _split3((-LOG2E) * cum)], axis=1)
        for j in range(sel_ref.shape[1] // TN_PROJ):
            cols = slice(j * TN_PROJ, (j + 1) * TN_PROJ)
            kb_ref[:, cols] = (jnp.dot(terms, sel_ref[:, cols], preferred_element_type=F32)
                               + ones_ref[:, cols]).astype(kb_ref.dtype)


def _fox_bias_selection():
    sel = np.zeros((3 * HEAD_BLOCK, D_MODEL), np.float32)
    ones = np.zeros((1, D_MODEL), np.float32)
    for hp in range(N_HEAD_BLOCKS):
        for a in range(2):
            for term in range(3):
                sel[term * HEAD_BLOCK + 2 * hp + a, hp * HEAD_BLOCK + 6 * a + term] = 1.0
            ones[0, hp * HEAD_BLOCK + 6 * a + 3:hp * HEAD_BLOCK + 6 * a + 6] = 1.0
    return jnp.asarray(sel, BF16), jnp.asarray(ones, F32)


def _modproj(x, shift, scale, w_std, w_t, seq, forget=None):
    t, d = x.shape
    tm = TM_PROJ
    tiles_per_seq = seq // tm
    const = lambda i: (0, 0)
    mod_spec = pl.BlockSpec((1, 1, d), lambda i: (i // tiles_per_seq, 0, 0))
    in_specs = [pl.BlockSpec((tm, d), lambda i: (i, 0)), mod_spec, mod_spec]
    in_specs += [pl.BlockSpec(w.shape, const) for w in w_std]
    in_specs += [pl.BlockSpec(w.shape, const) for w in w_t]
    out_shape = [jax.ShapeDtypeStruct((t, w.shape[1]), BF16) for w in w_std]
    out_shape += [jax.ShapeDtypeStruct((w.shape[0], t), BF16) for w in w_t]
    out_specs = [pl.BlockSpec((tm, w.shape[1]), lambda i: (i, 0)) for w in w_std]
    out_specs += [pl.BlockSpec((w.shape[0], tm), lambda i: (0, i)) for w in w_t]
    args = [x, shift, scale, *w_std, *w_t]
    scratch = []
    if forget is not None:
        wf, bf = forget
        sel, ones = _fox_bias_selection()
        in_specs += [pl.BlockSpec(a.shape, const) for a in (wf, bf, sel, ones)]
        args += [wf, bf, sel, ones]
        out_shape += [jax.ShapeDtypeStruct((t, d), BF16), jax.ShapeDtypeStruct((FOX_HEADS, t), F32)]
        out_specs += [pl.BlockSpec((tm, d), lambda i: (i, 0)),
                      pl.BlockSpec((FOX_HEADS, tm), lambda i: (0, i))]
        scratch = [pltpu.VMEM((1, HEAD_BLOCK), F32)]
    return pl.pallas_call(
        functools.partial(_modproj_kernel, n_std=len(w_std), n_t=len(w_t),
                          with_forget=forget is not None, tiles_per_seq=tiles_per_seq),
        out_shape=tuple(out_shape),
        grid=(t // tm,),
        in_specs=in_specs,
        out_specs=tuple(out_specs),
        scratch_shapes=scratch,
        compiler_params=_params("arbitrary"),
        name="modproj_fox" if forget is not None else "modproj",
    )(*args)


def _bias_rows(entries, tq):
    rid = lax.broadcasted_iota(jnp.int32, (BIAS_ROWS, tq), 0)
    out = jnp.zeros((BIAS_ROWS, tq), F32)
    for r, v in entries.items():
        out = jnp.where(rid == r, v, out)
    return out


def _attn_kernel(*refs, mode, lam_init, n_heads):
    if mode == "fox":
        q_ref, k_ref, kb_ref, v_ref, cumt_ref, o_ref, m_ref, acc_ref = refs
    else:
        q_ref, k_ref, kb_ref, v_ref, lam_ref, g_ref, o_ref, m_ref, acc_ref = refs
    hp = pl.program_id(1)
    i = pl.program_id(2)
    tq = q_ref.shape[1]

    qs = q_ref[...] * jnp.asarray(HALF ** -0.5 * LOG2E, BF16)
    zeros_half = jnp.zeros((HALF, tq), BF16)
    q_tops = (jnp.concatenate([qs[:HALF], zeros_half], axis=0),
              jnp.concatenate([zeros_half, qs[HALF:]], axis=0))
    pad = jnp.zeros((HEAD_BLOCK - BIAS_ROWS, tq), BF16)
    if mode == "fox":
        q_bias = []
        for a in range(2):
            c = _split3(cumt_ref[0, a:a + 1, :] * LOG2E)
            base = 6 * a
            rows = {base: 1.0, base + 1: 1.0, base + 2: 1.0,
                    base + 3: c[0], base + 4: c[1], base + 5: c[2]}
            q_bias.append(_bias_rows(rows, tq).astype(BF16))
    else:
        slope = jnp.exp2(jnp.full((1, 1), -8.0, F32) * (hp + 1).astype(F32) / n_heads)
        l3 = _split3(slope * LOG2E)
        qpos = (i * tq + lax.broadcasted_iota(jnp.int32, (1, tq), 1)).astype(F32)
        t3 = _split3(-(slope * LOG2E) * qpos)
        rows = {0: l3[0], 1: l3[0], 2: l3[1], 3: l3[1], 4: l3[2], 5: l3[2],
                6: t3[0], 7: t3[1], 8: t3[2]}
        q_bias = [_bias_rows(rows, tq).astype(BF16)] * 2
    q_aug = tuple(jnp.concatenate([q_tops[a], q_bias[a], pad], axis=0) for a in range(2))

    m_ref[...] = jnp.full_like(m_ref, NEG_INF)
    acc_ref[...] = jnp.zeros_like(acc_ref)
    ones_rows = jnp.ones((ONES_ROWS, TK), BF16)

    def key_slice(j):
        return pl.ds(pl.multiple_of(j * TK, TK), TK)

    def q_lanes(qc):
        return slice(qc * QUERY_CHUNK, (qc + 1) * QUERY_CHUNK)

    def scores(j, a, qc):
        ks = key_slice(j)
        k_aug = jnp.concatenate([k_ref[ks, :], kb_ref[ks, :]], axis=1)
        return jnp.dot(k_aug, q_aug[a][:, q_lanes(qc)], preferred_element_type=F32)

    v_rows = [slice(a * HALF, (a + 1) * HALF) if mode == "fox" else slice(0, HEAD_BLOCK)
              for a in range(2)]
    n_v = v_rows[0].stop - v_rows[0].start

    def accumulate(j, a, qc, s, masked):
        ql = q_lanes(qc)
        v_aug = jnp.concatenate([v_ref[v_rows[a], key_slice(j)], ones_rows], axis=0)
        if masked:
            keep = ((j * TK + lax.broadcasted_iota(jnp.int32, s.shape, 0))
                    <= (i * tq + qc * QUERY_CHUNK + lax.broadcasted_iota(jnp.int32, s.shape, 1)))
            s = jnp.where(keep, s, NEG_INF)
        m_old = m_ref[a, :, ql]
        m_new = jnp.maximum(m_old, jnp.max(s, axis=0, keepdims=True))
        alpha = jnp.exp2(m_old - m_new)
        p = jnp.exp2(s - m_new).astype(BF16)
        acc_ref[a, :, ql] = alpha * acc_ref[a, :, ql] + jnp.dot(v_aug, p, preferred_element_type=F32)
        m_ref[a, :, ql] = m_new

    def process(blocks):
        work = [(j, a, qc, masked) for (j, masked) in blocks for a in range(2)
                for qc in range(tq // QUERY_CHUNK)]
        pending = [scores(*w[:3]) for w in work[:LOOKAHEAD]]
        for n, (j, a, qc, masked) in enumerate(work):
            s = pending.pop(0)
            if n + LOOKAHEAD < len(work):
                pending.append(scores(*work[n + LOOKAHEAD][:3]))
            accumulate(j, a, qc, s, masked)

    n_full = (i * tq) // TK

    def pair_body(p, carry):
        process([(2 * p, False), (2 * p + 1, False)])
        return carry

    lax.fori_loop(0, n_full // 2, pair_body, 0)

    @pl.when(n_full % 2 == 1)
    def _():
        process([(n_full - 1, False), (n_full, True)])

    @pl.when(n_full % 2 == 0)
    def _():
        process([(n_full, True)])

    inv = [1.0 / acc_ref[a, n_v:n_v + 1, :] for a in range(2)]
    if mode == "fox":
        o_t = jnp.concatenate([acc_ref[a, :n_v, :] * inv[a] for a in range(2)], axis=0)
        o_ref[...] = o_t.T.astype(o_ref.dtype)
    else:
        lp = lam_ref[0]
        lam = (jnp.exp(jnp.sum(lp[0:1] * lp[1:2], axis=-1, keepdims=True))
               - jnp.exp(jnp.sum(lp[2:3] * lp[3:4], axis=-1, keepdims=True)) + lam_init)
        o_t = acc_ref[0, :HEAD_BLOCK, :] * inv[0] - lam * (acc_ref[1, :HEAD_BLOCK, :] * inv[1])
        o = o_t.T
        o = o * lax.rsqrt(jnp.mean(o * o, axis=-1, keepdims=True) + LN_EPS)
        o_ref[...] = (o * g_ref[...] * (1.0 - lam_init)).astype(o_ref.dtype)


def _attention(mode, q_t, k_arr, k_off, kb, kb_spec, v_t, v_off, extras, extra_specs, batch, seq,
               lam_init=0.0, n_heads=DIFF_HEADS):
    assert TQ == TK, "one masked diagonal block per query tile"
    nq = seq // TQ
    t = batch * seq
    acc_rows = (HALF if mode == "fox" else HEAD_BLOCK) + ONES_ROWS
    in_specs = [
        pl.BlockSpec((HEAD_BLOCK, TQ), lambda b, h, i: (h, b * nq + i)),
        pl.BlockSpec((seq, HEAD_BLOCK), lambda b, h, i: (b, k_off + h)),
        kb_spec,
        pl.BlockSpec((HEAD_BLOCK, seq), lambda b, h, i: (v_off + h, b)),
    ] + extra_specs
    return pl.pallas_call(
        functools.partial(_attn_kernel, mode=mode, lam_init=lam_init, n_heads=n_heads),
        out_shape=jax.ShapeDtypeStruct((t, D_MODEL), BF16),
        grid=(batch, N_HEAD_BLOCKS, nq),
        in_specs=in_specs,
        out_specs=pl.BlockSpec((TQ, HEAD_BLOCK), lambda b, h, i: (b * nq + i, h)),
        scratch_shapes=[pltpu.VMEM((2, 1, TQ), F32),
                        pltpu.VMEM((2, acc_rows, TQ), F32)],
        compiler_params=_params("parallel", "parallel", "arbitrary"),
        name="attn_" + mode,
    )(q_t, k_arr, kb, v_t, *extras)


def _bias_lanes(cols):
    lanes = jnp.stack(cols, axis=-1)
    pad = [(0, 0)] * (lanes.ndim - 1) + [(0, HEAD_BLOCK - lanes.shape[-1])]
    return jnp.pad(lanes, pad).astype(BF16)


def _alibi_key_bias(seq):
    kpos = jnp.arange(seq, dtype=jnp.int32)
    hi = ((kpos // 64) * 64).astype(F32)
    lo = (kpos % 64).astype(F32)
    one = jnp.ones((seq,), F32)
    return _bias_lanes([hi, lo, hi, lo, hi, lo, one, one, one])


SUBLANES = 8
LANES = 128


def _store_token_tiled(ref, value):
    n = value.shape[0]
    for s in range(SUBLANES):
        ref[pl.ds(s, n, stride=SUBLANES), :] = value[:, s * LANES:(s + 1) * LANES]


def _load_token_tiled(ref, n):
    return jnp.concatenate([ref[pl.ds(s, n, stride=SUBLANES), :] for s in range(SUBLANES)], axis=1)


def _layer_norm(z, g, b):
    mu = jnp.mean(z, axis=-1, keepdims=True)
    zc = z - mu
    var = jnp.mean(zc * zc, axis=-1, keepdims=True)
    return zc * lax.rsqrt(var + LN_EPS) * g + b


def _argmax_first(vals):
    best, idx = vals[0], jnp.zeros(vals[0].shape, jnp.int32)
    for j in range(1, len(vals)):
        better = vals[j] > best
        idx = jnp.where(better, j, idx)
        best = jnp.where(better, vals[j], best)
    return idx


def _select(vals, idx):
    out = vals[0]
    for j in range(1, len(vals)):
        out = jnp.where(idx == j, vals[j], out)
    return out


def _post_attn_kernel(o_ref, x_ref, wo_ref, ga_ref, lng_ref, lnb_ref, shf_ref, scf_ref,
                      rwt_ref, rb_ref, upper_ref, xo_ref, h_ref, eidx_ref, wts_ref, rank_ref,
                      counts_ref, count_ref):
    y = jnp.dot(o_ref[...], wo_ref[...], preferred_element_type=F32)
    z = DEEPNORM_ALPHA * x_ref[...] + (1.0 + ga_ref[0]) * y
    xn = _layer_norm(z, lng_ref[...], lnb_ref[...])
    xo_ref[...] = xn
    h = xn * (1.0 + scf_ref[0]) + shf_ref[0]
    _store_token_tiled(h_ref, h)

    logits = lax.dot_general(rwt_ref[...], h, NT_DIMS, precision=HIGHEST,
                             preferred_element_type=F32)
    rows = [logits[e:e + 1, :] for e in range(N_EXPERTS)]
    mx = functools.reduce(jnp.maximum, rows)
    ex = [jnp.exp(r - mx) for r in rows]
    den = functools.reduce(lambda a, b: a + b, ex)
    probs = [e_ / den for e_ in ex]
    sel = [probs[e] + rb_ref[e:e + 1, :] for e in range(N_EXPERTS)]

    grp_scores = []
    for g in range(N_GROUPS):
        v = sel[g * EXPERTS_PER_GROUP:(g + 1) * EXPERTS_PER_GROUP]
        pair_sums = [v[a] + v[b] for a in range(EXPERTS_PER_GROUP) for b in range(a + 1, EXPERTS_PER_GROUP)]
        grp_scores.append(functools.reduce(jnp.maximum, pair_sums))
    g_idx = _argmax_first(grp_scores)

    in_sel = [_select([sel[g * EXPERTS_PER_GROUP + j] for g in range(N_GROUPS)], g_idx)
              for j in range(EXPERTS_PER_GROUP)]
    in_prob = [_select([probs[g * EXPERTS_PER_GROUP + j] for g in range(N_GROUPS)], g_idx)
               for j in range(EXPERTS_PER_GROUP)]
    i1 = _argmax_first(in_sel)
    b2 = jnp.full(in_sel[0].shape, -jnp.inf, F32)
    i2 = jnp.zeros(in_sel[0].shape, jnp.int32)
    for j in range(EXPERTS_PER_GROUP):
        cand = (i1 != j) & (in_sel[j] > b2)
        i2 = jnp.where(cand, j, i2)
        b2 = jnp.where(cand, in_sel[j], b2)
    p1 = _select(in_prob, i1)
    p2 = _select(in_prob, i2)
    tot = p1 + p2
    experts = (g_idx * EXPERTS_PER_GROUP + i1, g_idx * EXPERTS_PER_GROUP + i2)
    wts_ref[0:1, :] = p1 / tot
    wts_ref[1:2, :] = p2 / tot

    @pl.when(pl.program_id(0) == 0)
    def _():
        count_ref[...] = jnp.zeros_like(count_ref)

    tm = experts[0].shape[1]
    eid = lax.broadcasted_iota(jnp.int32, (N_EXPERTS, tm), 0)
    hit = [eid == e for e in experts]
    chosen = jnp.where(hit[0] | hit[1], 1.0, 0.0)
    before = jnp.dot(chosen.astype(BF16), upper_ref[...], preferred_element_type=F32) + count_ref[...]
    for k in range(TOP_K):
        eidx_ref[k:k + 1, :] = experts[k]
        rank_ref[k:k + 1, :] = jnp.sum(jnp.where(hit[k], before, 0.0), axis=0,
                                       keepdims=True).astype(jnp.int32)
    count_ref[...] += jnp.sum(chosen, axis=1, keepdims=True)
    counts_ref[...] = jnp.broadcast_to(count_ref[...], counts_ref.shape)


def _post_attn(o, x, wo, g_a, ln_g, ln_b, sh_f, sc_f, router_wt, router_b, seq):
    t, d = x.shape
    tm = TM_PROJ
    tiles_per_seq = seq // tm
    mod_spec = pl.BlockSpec((1, 1, d), lambda i: (i // tiles_per_seq, 0, 0))
    vec_spec = pl.BlockSpec((1, d), lambda i: (0, 0))
    tok_spec = pl.BlockSpec((tm, d), lambda i: (i, 0))
    route_spec = pl.BlockSpec((TOP_K, tm), lambda i: (0, i))
    upper = jnp.asarray(np.triu(np.ones((tm, tm), np.float32), k=1), BF16)
    return pl.pallas_call(
        _post_attn_kernel,
        out_shape=(jax.ShapeDtypeStruct((t, d), F32), jax.ShapeDtypeStruct((t * SUBLANES, LANES), F32),
                   jax.ShapeDtypeStruct((TOP_K, t), jnp.int32), jax.ShapeDtypeStruct((TOP_K, t), F32),
                   jax.ShapeDtypeStruct((TOP_K, t), jnp.int32),
                   jax.ShapeDtypeStruct((N_EXPERTS, LANES), F32)),
        grid=(t // tm,),
        in_specs=[
            tok_spec, tok_spec,
            pl.BlockSpec((d, d), lambda i: (0, 0)),
            mod_spec, vec_spec, vec_spec, mod_spec, mod_spec,
            pl.BlockSpec((N_EXPERTS, d), lambda i: (0, 0)),
            pl.BlockSpec((N_EXPERTS, 1), lambda i: (0, 0)),
            pl.BlockSpec((tm, tm), lambda i: (0, 0)),
        ],
        out_specs=(tok_spec, pl.BlockSpec((tm * SUBLANES, LANES), lambda i: (i, 0)),
                   route_spec, route_spec, route_spec,
                   pl.BlockSpec((N_EXPERTS, LANES), lambda i: (0, 0))),
        scratch_shapes=[pltpu.VMEM((N_EXPERTS, 1), F32)],
        compiler_params=_params("arbitrary"),
        name="post_attn",
    )(o, x, wo, g_a, ln_g, ln_b, sh_f, sc_f, router_wt, router_b, upper)


def _routing_tables(eidx, rank, counts, rows_pad):
    t = eidx.shape[1]
    padded = ((counts + TM_EXPERT - 1) // TM_EXPERT) * TM_EXPERT
    ends = jnp.cumsum(padded)
    starts = ends - padded
    pos = (jnp.take(starts, eidx) + rank).reshape(-1)
    tok = jnp.tile(jnp.arange(t, dtype=jnp.int32), TOP_K)
    sorted_tok = (jnp.arange(rows_pad, dtype=jnp.int32) % t).at[pos].set(tok)
    n_tiles = rows_pad // TM_EXPERT
    tile_start = jnp.arange(n_tiles, dtype=jnp.int32) * TM_EXPERT
    tile_expert = jnp.minimum(jnp.sum((tile_start[:, None] >= ends[None, :]).astype(jnp.int32), axis=1),
                              N_EXPERTS - 1).astype(jnp.int32)
    n_valid = (ends[-1] // TM_EXPERT).astype(jnp.int32).reshape(1)
    return pos.astype(jnp.int32), sorted_tok, tile_expert, n_valid


def _row_copy(src_hbm, src_row8, dst, dst_row, sem):
    return pltpu.make_async_copy(
        src_hbm.at[pl.ds(pl.multiple_of(src_row8, SUBLANES), SUBLANES)],
        dst.at[pl.ds(pl.multiple_of(dst_row * SUBLANES, SUBLANES), SUBLANES)], sem)


def _expert_kernel(te_ref, nv_ref, tok8_ref, h_hbm, wg_ref, wu_ref, wd_ref, y_ref, buf, sem):
    i = pl.program_id(0)
    nv = nv_ref[0]
    slot = i % 2

    def issue(tile, slot_):
        base = tile * TM_EXPERT

        def body(r, carry):
            _row_copy(h_hbm, tok8_ref[base + r], buf.at[slot_], r, sem.at[slot_]).start()
            return carry

        lax.fori_loop(0, TM_EXPERT, body, 0, unroll=8)

    @pl.when((i == 0) & (nv > 0))
    def _():
        issue(0, 0)

    @pl.when(i + 1 < nv)
    def _():
        issue(i + 1, 1 - slot)

    @pl.when(i < nv)
    def _():
        pltpu.make_async_copy(h_hbm.at[pl.ds(0, TM_EXPERT * SUBLANES)], buf.at[slot],
                              sem.at[slot]).wait()
        hb = _load_token_tiled(buf.at[slot], TM_EXPERT).astype(BF16)
        g = jnp.dot(hb, wg_ref[0], preferred_element_type=F32)
        u = jnp.dot(hb, wu_ref[0], preferred_element_type=F32)
        a = g * (1.0 / (1.0 + jnp.exp(-g))) * u
        _store_token_tiled(y_ref, jnp.dot(a.astype(BF16), wd_ref[0], preferred_element_type=F32))

    @pl.when(i >= nv)
    def _():
        y_ref[...] = jnp.zeros_like(y_ref)


def _experts(h_tiled, tile_expert, n_valid, sorted_tok8, wg, wu, wd):
    d = wg.shape[1]
    rows_pad = sorted_tok8.shape[0]
    n_tiles = rows_pad // TM_EXPERT
    grid_spec = pltpu.PrefetchScalarGridSpec(
        num_scalar_prefetch=3,
        grid=(n_tiles,),
        in_specs=[
            pl.BlockSpec(memory_space=pl.ANY),
            pl.BlockSpec((1, d, D_EXPERT), lambda i, te, nv, tok: (te[i], 0, 0)),
            pl.BlockSpec((1, d, D_EXPERT), lambda i, te, nv, tok: (te[i], 0, 0)),
            pl.BlockSpec((1, D_EXPERT, d), lambda i, te, nv, tok: (te[i], 0, 0)),
        ],
        out_specs=pl.BlockSpec((TM_EXPERT * SUBLANES, LANES), lambda i, te, nv, tok: (i, 0)),
        scratch_shapes=[pltpu.VMEM((2, TM_EXPERT * SUBLANES, LANES), F32),
                        pltpu.SemaphoreType.DMA((2,))],
    )
    return pl.pallas_call(
        _expert_kernel,
        out_shape=jax.ShapeDtypeStruct((rows_pad * SUBLANES, LANES), F32),
        grid_spec=grid_spec,
        compiler_params=_params("arbitrary"),
        name="experts",
    )(tile_expert, n_valid, sorted_tok8, h_tiled, wg, wu, wd)


def _combine_kernel(pos_ref, y_hbm, x_ref, w_ref, gf_ref, lng_ref, lnb_ref, o_ref, buf, sem, *, n_tok):
    i = pl.program_id(0)
    n = pl.num_programs(0)
    slot = i % 2
    tm = x_ref.shape[0]

    def issue(tile, slot_):
        base = tile * tm

        def body(r, carry):
            for k in range(TOP_K):
                _row_copy(y_hbm, pos_ref[k * n_tok + base + r], buf.at[slot_, k], r,
                          sem.at[slot_]).start()
            return carry

        lax.fori_loop(0, tm, body, 0, unroll=8)

    @pl.when(i == 0)
    def _():
        issue(0, 0)

    @pl.when(i + 1 < n)
    def _():
        issue(i + 1, 1 - slot)

    for k in range(TOP_K):
        pltpu.make_async_copy(y_hbm.at[pl.ds(0, tm * SUBLANES)], buf.at[slot, k], sem.at[slot]).wait()
    w = w_ref[...]
    m = (w[:, 0:1] * _load_token_tiled(buf.at[slot, 0], tm)
         + w[:, 1:2] * _load_token_tiled(buf.at[slot, 1], tm))
    z = DEEPNORM_ALPHA * x_ref[...] + (1.0 + gf_ref[0]) * m
    o_ref[...] = _layer_norm(z, lng_ref[...], lnb_ref[...])


def _combine(pos, y, x, wts_t, g_f, ln_g, ln_b, seq):
    t, d = x.shape
    tm = TM_COMBINE
    tiles_per_seq = seq // tm
    grid_spec = pltpu.PrefetchScalarGridSpec(
        num_scalar_prefetch=1,
        grid=(t // tm,),
        in_specs=[
            pl.BlockSpec(memory_space=pl.ANY),
            pl.BlockSpec((tm, d), lambda i, pos_: (i, 0)),
            pl.BlockSpec((tm, TOP_K), lambda i, pos_: (i, 0)),
            pl.BlockSpec((1, 1, d), lambda i, pos_: (i // tiles_per_seq, 0, 0)),
            pl.BlockSpec((1, d), lambda i, pos_: (0, 0)),
            pl.BlockSpec((1, d), lambda i, pos_: (0, 0)),
        ],
        out_specs=pl.BlockSpec((tm, d), lambda i, pos_: (i, 0)),
        scratch_shapes=[pltpu.VMEM((2, TOP_K, tm * SUBLANES, LANES), F32),
                        pltpu.SemaphoreType.DMA((2,))],
    )
    return pl.pallas_call(
        functools.partial(_combine_kernel, n_tok=t),
        out_shape=jax.ShapeDtypeStruct((t, d), F32),
        grid_spec=grid_spec,
        compiler_params=_params("arbitrary"),
        name="combine",
    )(pos, y, x, wts_t, g_f, ln_g, ln_b)


def kernel(x, c, ada_w, ada_b, ln_attn_g, ln_attn_b, ln_ffn_g, ln_ffn_b, fox_w_in, fox_b_f, fox_w_o,
           kv_ada_w, kv_ada_b, kv_w, diff_w_q, diff_lambda, diff_norm_g, diff_w_o, router_w, router_b,
           moe_w_gate, moe_w_up, moe_w_down):
    bsz, seq, d = x.shape
    t = bsz * seq
    rows_pad = TOP_K * t + N_EXPERTS * TM_EXPERT
    nq = seq // TQ

    mod = _ada(c, ada_w, ada_b, 1536)
    kv_mod = _ada(c, kv_ada_w[None], kv_ada_b[None], 1024)[0]

    def mod_part(l, k):
        return mod[l, :, k * d:(k + 1) * d].reshape(bsz, 1, d)

    router_wt = router_w.T
    router_bc = router_b.reshape(N_EXPERTS, 1)
    xt = x.reshape(t, d)

    k_sh = v_sh_t = None
    alibi_kb = _alibi_key_bias(seq)
    for l in range(DEPTH):
        sh_a, sc_a, g_a, sh_f, sc_f, g_f = (mod_part(l, k) for k in range(6))
        if l < N_A:
            w_in = fox_w_in[l]
            w_qt = w_in[:, :d].T.astype(BF16)
            w_k = w_in[:, d:2 * d].astype(BF16)
            w_vt = w_in[:, 2 * d:3 * d].T.astype(BF16)
            w_f = jnp.pad(w_in[:, 3 * d:], ((0, 0), (0, HEAD_BLOCK - FOX_HEADS))).astype(BF16)
            b_f = jnp.pad(fox_b_f[l], (0, HEAD_BLOCK - FOX_HEADS)).reshape(1, HEAD_BLOCK)
            k_arr, q_t, v_t, key_bias, cum_t = _modproj(xt, sh_a, sc_a, [w_k], [w_qt, w_vt], seq,
                                                        forget=(w_f, b_f))
            o = _attention(
                "fox", q_t, k_arr, 0, key_bias,
                pl.BlockSpec((seq, HEAD_BLOCK), lambda b, h, i: (b, h)),
                v_t, 0, (cum_t.reshape(N_HEAD_BLOCKS, 2, t),),
                [pl.BlockSpec((1, 2, TQ), lambda b, h, i: (h, 0, b * nq + i))],
                bsz, seq)
            w_o = fox_w_o[l].astype(BF16)
        else:
            j = l - N_A
            if l == N_A:
                k_sh, v_sh_t = _modproj(xt, kv_mod[:, :d].reshape(bsz, 1, d),
                                        kv_mod[:, d:].reshape(bsz, 1, d),
                                        [kv_w[:, :d].astype(BF16)], [kv_w[:, d:].T.astype(BF16)], seq)
            (q_t,) = _modproj(xt, sh_a, sc_a, [], [diff_w_q[j].T.astype(BF16)], seq)
            lam_init = 0.8 - 0.6 * math.exp(-0.3 * l)
            o = _attention(
                "diff", q_t, k_sh, 0, alibi_kb,
                pl.BlockSpec((seq, HEAD_BLOCK), lambda b, h, i: (0, 0)),
                v_sh_t, 0,
                (diff_lambda[j][None], diff_norm_g[j].reshape(1, HEAD_BLOCK)),
                [pl.BlockSpec((1, 4, HALF), lambda b, h, i: (0, 0, 0)),
                 pl.BlockSpec((1, HEAD_BLOCK), lambda b, h, i: (0, 0))],
                bsz, seq, lam_init=lam_init)
            w_o = diff_w_o[j].astype(BF16)

        xt, h, eidx, wts, rank, counts = _post_attn(
            o, xt, w_o, g_a, ln_attn_g[l].reshape(1, d), ln_attn_b[l].reshape(1, d), sh_f, sc_f,
            router_wt, router_bc, seq)
        pos, sorted_tok, tile_expert, n_valid = _routing_tables(
            eidx, rank, counts[:, 0].astype(jnp.int32), rows_pad)
        y = _experts(h, tile_expert, n_valid, sorted_tok * SUBLANES, moe_w_gate[l].astype(BF16),
                     moe_w_up[l].astype(BF16), moe_w_down[l].astype(BF16))
        xt = _combine(pos * SUBLANES, y, xt, wts.T, g_f, ln_ffn_g[l].reshape(1, d),
                      ln_ffn_b[l].reshape(1, d), seq)
    return xt.reshape(bsz, seq, d)
```

```python
import functools
import math

import jax
import jax.numpy as jnp
import numpy as np
from jax import lax
from jax.experimental import pallas as pl
from jax.experimental.pallas import tpu as pltpu

F32 = jnp.float32
BF16 = jnp.bfloat16
HIGHEST = lax.Precision.HIGHEST

D_MODEL = 1024
DEPTH = 4
N_A = DEPTH // 2
HEAD_BLOCK = 128
N_HEAD_BLOCKS = D_MODEL // HEAD_BLOCK
HALF = HEAD_BLOCK // 2
FOX_HEADS = 16
DIFF_HEADS = 8
N_EXPERTS = 16
N_GROUPS = 4
EXPERTS_PER_GROUP = N_EXPERTS // N_GROUPS
TOP_K = 2
D_EXPERT = 512
DEEPNORM_ALPHA = (2 * DEPTH) ** 0.25
LN_EPS = 1e-5
NEG_INF = -1e30
LOG2E = math.log2(math.e)

VMEM_LIMIT = 56 * 1024 * 1024

TM_PROJ = 512
TN_PROJ = 256
TQ = 512
TK = 512
QUERY_CHUNK = 512
LOOKAHEAD = 3
BIAS_ROWS = 16
ONES_ROWS = 16
TM_EXPERT = 512
TM_COMBINE = 256
TM_DISPATCH = 1024

NT_DIMS = (((1,), (1,)), ((), ()))


def _params(*sem, flags=None):
    return pltpu.CompilerParams(dimension_semantics=sem, vmem_limit_bytes=VMEM_LIMIT, flags=flags)


def _split3(v):
    hi = v.astype(BF16).astype(F32)
    r = v - hi
    mid = r.astype(BF16).astype(F32)
    lo = (r - mid).astype(BF16).astype(F32)
    return hi, mid, lo


def _ada_kernel(c_ref, w_ref, b_ref, o_ref):
    c = c_ref[...]
    sc = c * (1.0 / (1.0 + jnp.exp(-c)))
    o_ref[0] = jnp.dot(sc, w_ref[0], precision=HIGHEST, preferred_element_type=F32) + b_ref[0]


def _ada(c, w, b, tn):
    n_layers, d, n = w.shape
    bsz = c.shape[0]
    return pl.pallas_call(
        _ada_kernel,
        out_shape=jax.ShapeDtypeStruct((n_layers, bsz, n), F32),
        grid=(n_layers, n // tn),
        in_specs=[
            pl.BlockSpec((bsz, d), lambda l, j: (0, 0)),
            pl.BlockSpec((1, d, tn), lambda l, j: (l, 0, j)),
            pl.BlockSpec((1, 1, tn), lambda l, j: (l, 0, j)),
        ],
        out_specs=pl.BlockSpec((1, bsz, tn), lambda l, j: (l, 0, j)),
        compiler_params=_params("parallel", "parallel"),
        name="ada_params",
    )(c, w, b.reshape(n_layers, 1, n))


def _modproj_kernel(*refs, n_std, n_t, with_forget, tiles_per_seq):
    x_ref, sh_ref, sc_ref = refs[:3]
    pos = 3
    wstd = refs[pos:pos + n_std]; pos += n_std
    wt = refs[pos:pos + n_t]; pos += n_t
    if with_forget:
        wf_ref, bf_ref, sel_ref, ones_ref = refs[pos:pos + 4]; pos += 4
    ostd = refs[pos:pos + n_std]; pos += n_std
    ot = refs[pos:pos + n_t]; pos += n_t
    if with_forget:
        kb_ref, cumt_ref, carry_ref = refs[pos:pos + 3]

    hb = (x_ref[...] * (1.0 + sc_ref[0]) + sh_ref[0]).astype(BF16)
    for w_ref, o_ref in zip(wstd, ostd):
        for j in range(w_ref.shape[1] // TN_PROJ):
            cols = slice(j * TN_PROJ, (j + 1) * TN_PROJ)
            o_ref[:, cols] = jnp.dot(hb, w_ref[:, cols], preferred_element_type=F32).astype(o_ref.dtype)
    for w_ref, o_ref in zip(wt, ot):
        for j in range(w_ref.shape[0] // TN_PROJ):
            rows = slice(j * TN_PROJ, (j + 1) * TN_PROJ)
            o_ref[rows, :] = lax.dot_general(w_ref[rows, :], hb, NT_DIMS,
                                             preferred_element_type=F32).astype(o_ref.dtype)

    if with_forget:
        @pl.when(pl.program_id(0) % tiles_per_seq == 0)
        def _():
            carry_ref[...] = jnp.zeros_like(carry_ref)

        z = jnp.dot(hb, wf_ref[...], preferred_element_type=F32) + bf_ref[...]
        logf = jnp.minimum(z, 0.0) - jnp.log(1.0 + jnp.exp(-jnp.abs(z)))
        tm = z.shape[0]
        lower = (lax.broadcasted_iota(jnp.int32, (tm, tm), 0)
                 >= lax.broadcasted_iota(jnp.int32, (tm, tm), 1)).astype(F32)
        cum = jnp.dot(lower, logf, precision=HIGHEST, preferred_element_type=F32) + carry_ref[...]
        cumt_ref[...] = cum.T[:FOX_HEADS, :]
        carry_ref[...] = cum[tm - 1:tm, :]
        terms = jnp.concatenate([v.astype(BF16) for v in _split3((-LOG2E) * cum)], axis=1)
        for j in range(sel_ref.shape[1] // TN_PROJ):
            cols = slice(j * TN_PROJ, (j + 1) * TN_PROJ)
            kb_ref[:, cols] = (jnp.dot(terms, sel_ref[:, cols], preferred_element_type=F32)
                               + ones_ref[:, cols]).astype(kb_ref.dtype)


def _fox_bias_selection():
    sel = np.zeros((3 * HEAD_BLOCK, D_MODEL), np.float32)
    ones = np.zeros((1, D_MODEL), np.float32)
    for hp in range(N_HEAD_BLOCKS):
        for a in range(2):
            for term in range(3):
                sel[term * HEAD_BLOCK + 2 * hp + a, hp * HEAD_BLOCK + 6 * a + term] = 1.0
            ones[0, hp * HEAD_BLOCK + 6 * a + 3:hp * HEAD_BLOCK + 6 * a + 6] = 1.0
    return jnp.asarray(sel, BF16), jnp.asarray(ones, F32)


def _modproj(x, shift, scale, w_std, w_t, seq, forget=None):
    t, d = x.shape
    tm = TM_PROJ
    tiles_per_seq = seq // tm
    const = lambda i: (0, 0)
    mod_spec = pl.BlockSpec((1, 1, d), lambda i: (i // tiles_per_seq, 0, 0))
    in_specs = [pl.BlockSpec((tm, d), lambda i: (i, 0)), mod_spec, mod_spec]
    in_specs += [pl.BlockSpec(w.shape, const) for w in w_std]
    in_specs += [pl.BlockSpec(w.shape, const) for w in w_t]
    out_shape = [jax.ShapeDtypeStruct((t, w.shape[1]), BF16) for w in w_std]
    out_shape += [jax.ShapeDtypeStruct((w.shape[0], t), BF16) for w in w_t]
    out_specs = [pl.BlockSpec((tm, w.shape[1]), lambda i: (i, 0)) for w in w_std]
    out_specs += [pl.BlockSpec((w.shape[0], tm), lambda i: (0, i)) for w in w_t]
    args = [x, shift, scale, *w_std, *w_t]
    scratch = []
    if forget is not None:
        wf, bf = forget
        sel, ones = _fox_bias_selection()
        in_specs += [pl.BlockSpec(a.shape, const) for a in (wf, bf, sel, ones)]
        args += [wf, bf, sel, ones]
        out_shape += [jax.ShapeDtypeStruct((t, d), BF16), jax.ShapeDtypeStruct((FOX_HEADS, t), F32)]
        out_specs += [pl.BlockSpec((tm, d), lambda i: (i, 0)),
                      pl.BlockSpec((FOX_HEADS, tm), lambda i: (0, i))]
        scratch = [pltpu.VMEM((1, HEAD_BLOCK), F32)]
    return pl.pallas_call(
        functools.partial(_modproj_kernel, n_std=len(w_std), n_t=len(w_t),
                          with_forget=forget is not None, tiles_per_seq=tiles_per_seq),
        out_shape=tuple(out_shape),
        grid=(t // tm,),
        in_specs=in_specs,
        out_specs=tuple(out_specs),
        scratch_shapes=scratch,
        compiler_params=_params("arbitrary"),
        name="modproj_fox" if forget is not None else "modproj",
    )(*args)


def _bias_rows(entries, tq):
    rid = lax.broadcasted_iota(jnp.int32, (BIAS_ROWS, tq), 0)
    out = jnp.zeros((BIAS_ROWS, tq), F32)
    for r, v in entries.items():
        out = jnp.where(rid == r, v, out)
    return out


def _attn_kernel(*refs, mode, lam_init, n_heads):
    if mode == "fox":
        q_ref, k_ref, kb_ref, v_ref, cumt_ref, o_ref, m_ref, acc_ref = refs
    else:
        q_ref, k_ref, kb_ref, v_ref, lam_ref, g_ref, o_ref, m_ref, acc_ref = refs
    hp = pl.program_id(1)
    i = pl.program_id(2)
    tq = q_ref.shape[1]

    qs = q_ref[...] * jnp.asarray(HALF ** -0.5 * LOG2E, BF16)
    zeros_half = jnp.zeros((HALF, tq), BF16)
    q_tops = (jnp.concatenate([qs[:HALF], zeros_half], axis=0),
              jnp.concatenate([zeros_half, qs[HALF:]], axis=0))
    pad = jnp.zeros((HEAD_BLOCK - BIAS_ROWS, tq), BF16)
    if mode == "fox":
        q_bias = []
        for a in range(2):
            c = _split3(cumt_ref[0, a:a + 1, :] * LOG2E)
            base = 6 * a
            rows = {base: 1.0, base + 1: 1.0, base + 2: 1.0,
                    base + 3: c[0], base + 4: c[1], base + 5: c[2]}
            q_bias.append(_bias_rows(rows, tq).astype(BF16))
    else:
        slope = jnp.exp2(jnp.full((1, 1), -8.0, F32) * (hp + 1).astype(F32) / n_heads)
        l3 = _split3(slope * LOG2E)
        qpos = (i * tq + lax.broadcasted_iota(jnp.int32, (1, tq), 1)).astype(F32)
        t3 = _split3(-(slope * LOG2E) * qpos)
        rows = {0: l3[0], 1: l3[0], 2: l3[1], 3: l3[1], 4: l3[2], 5: l3[2],
                6: t3[0], 7: t3[1], 8: t3[2]}
        q_bias = [_bias_rows(rows, tq).astype(BF16)] * 2
    q_aug = tuple(jnp.concatenate([q_tops[a], q_bias[a], pad], axis=0) for a in range(2))

    m_ref[...] = jnp.full_like(m_ref, NEG_INF)
    acc_ref[...] = jnp.zeros_like(acc_ref)
    ones_rows = jnp.ones((ONES_ROWS, TK), BF16)

    def key_slice(j):
        return pl.ds(pl.multiple_of(j * TK, TK), TK)

    def q_lanes(qc):
        return slice(qc * QUERY_CHUNK, (qc + 1) * QUERY_CHUNK)

    def scores(j, a, qc):
        ks = key_slice(j)
        k_aug = jnp.concatenate([k_ref[ks, :], kb_ref[ks, :]], axis=1)
        return jnp.dot(k_aug, q_aug[a][:, q_lanes(qc)], preferred_element_type=F32)

    v_rows = [slice(a * HALF, (a + 1) * HALF) if mode == "fox" else slice(0, HEAD_BLOCK)
              for a in range(2)]
    n_v = v_rows[0].stop - v_rows[0].start

    def accumulate(j, a, qc, s, masked):
        ql = q_lanes(qc)
        v_aug = jnp.concatenate([v_ref[v_rows[a], key_slice(j)], ones_rows], axis=0)
        if masked:
            keep = ((j * TK + lax.broadcasted_iota(jnp.int32, s.shape, 0))
                    <= (i * tq + qc * QUERY_CHUNK + lax.broadcasted_iota(jnp.int32, s.shape, 1)))
            s = jnp.where(keep, s, NEG_INF)
        m_old = m_ref[a, :, ql]
        m_new = jnp.maximum(m_old, jnp.max(s, axis=0, keepdims=True))
        alpha = jnp.exp2(m_old - m_new)
        p = jnp.exp2(s - m_new).astype(BF16)
        acc_ref[a, :, ql] = alpha * acc_ref[a, :, ql] + jnp.dot(v_aug, p, preferred_element_type=F32)
        m_ref[a, :, ql] = m_new

    def process(blocks):
        work = [(j, a, qc, masked) for (j, masked) in blocks for a in range(2)
                for qc in range(tq // QUERY_CHUNK)]
        pending = [scores(*w[:3]) for w in work[:LOOKAHEAD]]
        for n, (j, a, qc, masked) in enumerate(work):
            s = pending.pop(0)
            if n + LOOKAHEAD < len(work):
                pending.append(scores(*work[n + LOOKAHEAD][:3]))
            accumulate(j, a, qc, s, masked)

    n_full = (i * tq) // TK

    def pair_body(p, carry):
        process([(2 * p, False), (2 * p + 1, False)])
        return carry

    lax.fori_loop(0, n_full // 2, pair_body, 0)

    @pl.when(n_full % 2 == 1)
    def _():
        process([(n_full - 1, False), (n_full, True)])

    @pl.when(n_full % 2 == 0)
    def _():
        process([(n_full, True)])

    inv = [1.0 / acc_ref[a, n_v:n_v + 1, :] for a in range(2)]
    if mode == "fox":
        o_t = jnp.concatenate([acc_ref[a, :n_v, :] * inv[a] for a in range(2)], axis=0)
        o_ref[...] = o_t.T.astype(o_ref.dtype)
    else:
        lp = lam_ref[0]
        lam = (jnp.exp(jnp.sum(lp[0:1] * lp[1:2], axis=-1, keepdims=True))
               - jnp.exp(jnp.sum(lp[2:3] * lp[3:4], axis=-1, keepdims=True)) + lam_init)
        o_t = acc_ref[0, :HEAD_BLOCK, :] * inv[0] - lam * (acc_ref[1, :HEAD_BLOCK, :] * inv[1])
        o = o_t.T
        o = o * lax.rsqrt(jnp.mean(o * o, axis=-1, keepdims=True) + LN_EPS)
        o_ref[...] = (o * g_ref[...] * (1.0 - lam_init)).astype(o_ref.dtype)


def _attention(mode, q_t, k_arr, k_off, kb, kb_spec, v_t, v_off, extras, extra_specs, batch, seq,
               lam_init=0.0, n_heads=DIFF_HEADS):
    assert TQ == TK, "one masked diagonal block per query tile"
    nq = seq // TQ
    t = batch * seq
    acc_rows = (HALF if mode == "fox" else HEAD_BLOCK) + ONES_ROWS
    in_specs = [
        pl.BlockSpec((HEAD_BLOCK, TQ), lambda b, h, i: (h, b * nq + i)),
        pl.BlockSpec((seq, HEAD_BLOCK), lambda b, h, i: (b, k_off + h)),
        kb_spec,
        pl.BlockSpec((HEAD_BLOCK, seq), lambda b, h, i: (v_off + h, b)),
    ] + extra_specs
    return pl.pallas_call(
        functools.partial(_attn_kernel, mode=mode, lam_init=lam_init, n_heads=n_heads),
        out_shape=jax.ShapeDtypeStruct((t, D_MODEL), BF16),
        grid=(batch, N_HEAD_BLOCKS, nq),
        in_specs=in_specs,
        out_specs=pl.BlockSpec((TQ, HEAD_BLOCK), lambda b, h, i: (b * nq + i, h)),
        scratch_shapes=[pltpu.VMEM((2, 1, TQ), F32),
                        pltpu.VMEM((2, acc_rows, TQ), F32)],
        compiler_params=_params("parallel", "parallel", "arbitrary"),
        name="attn_" + mode,
    )(q_t, k_arr, kb, v_t, *extras)


def _bias_lanes(cols):
    lanes = jnp.stack(cols, axis=-1)
    pad = [(0, 0)] * (lanes.ndim - 1) + [(0, HEAD_BLOCK - lanes.shape[-1])]
    return jnp.pad(lanes, pad).astype(BF16)


def _alibi_key_bias(seq):
    kpos = jnp.arange(seq, dtype=jnp.int32)
    hi = ((kpos // 64) * 64).astype(F32)
    lo = (kpos % 64).astype(F32)
    one = jnp.ones((seq,), F32)
    return _bias_lanes([hi, lo, hi, lo, hi, lo, one, one, one])


SUBLANES = 8
LANES = 128


def _store_token_tiled(ref, value):
    n = value.shape[0]
    for s in range(SUBLANES):
        ref[pl.ds(s, n, stride=SUBLANES), :] = value[:, s * LANES:(s + 1) * LANES]


def _load_token_tiled(ref, n):
    return jnp.concatenate([ref[pl.ds(s, n, stride=SUBLANES), :] for s in range(SUBLANES)], axis=1)


def _layer_norm(z, g, b):
    mu = jnp.mean(z, axis=-1, keepdims=True)
    zc = z - mu
    var = jnp.mean(zc * zc, axis=-1, keepdims=True)
    return zc * lax.rsqrt(var + LN_EPS) * g + b


def _argmax_first(vals):
    best, idx = vals[0], jnp.zeros(vals[0].shape, jnp.int32)
    for j in range(1, len(vals)):
        better = vals[j] > best
        idx = jnp.where(better, j, idx)
        best = jnp.where(better, vals[j], best)
    return idx


def _select(vals, idx):
    out = vals[0]
    for j in range(1, len(vals)):
        out = jnp.where(idx == j, vals[j], out)
    return out


def _post_attn_kernel(o_ref, x_ref, wo_ref, ga_ref, lng_ref, lnb_ref, shf_ref, scf_ref,
                      rwt_ref, rb_ref, upper_ref, xo_ref, h_ref, eidx_ref, wts_ref, rank_ref,
                      counts_ref, count_ref):
    y = jnp.dot(o_ref[...], wo_ref[...], preferred_element_type=F32)
    z = DEEPNORM_ALPHA * x_ref[...] + (1.0 + ga_ref[0]) * y
    xn = _layer_norm(z, lng_ref[...], lnb_ref[...])
    xo_ref[...] = xn
    h = xn * (1.0 + scf_ref[0]) + shf_ref[0]
    _store_token_tiled(h_ref, h)

    logits = lax.dot_general(rwt_ref[...], h, NT_DIMS, precision=HIGHEST,
                             preferred_element_type=F32)
    rows = [logits[e:e + 1, :] for e in range(N_EXPERTS)]
    mx = functools.reduce(jnp.maximum, rows)
    ex = [jnp.exp(r - mx) for r in rows]
    den = functools.reduce(lambda a, b: a + b, ex)
    probs = [e_ / den for e_ in ex]
    sel = [probs[e] + rb_ref[e:e + 1, :] for e in range(N_EXPERTS)]

    grp_scores = []
    for g in range(N_GROUPS):
        v = sel[g * EXPERTS_PER_GROUP:(g + 1) * EXPERTS_PER_GROUP]
        pair_sums = [v[a] + v[b] for a in range(EXPERTS_PER_GROUP) for b in range(a + 1, EXPERTS_PER_GROUP)]
        grp_scores.append(functools.reduce(jnp.maximum, pair_sums))
    g_idx = _argmax_first(grp_scores)

    in_sel = [_select([sel[g * EXPERTS_PER_GROUP + j] for g in range(N_GROUPS)], g_idx)
              for j in range(EXPERTS_PER_GROUP)]
    in_prob = [_select([probs[g * EXPERTS_PER_GROUP + j] for g in range(N_GROUPS)], g_idx)
               for j in range(EXPERTS_PER_GROUP)]
    i1 = _argmax_first(in_sel)
    b2 = jnp.full(in_sel[0].shape, -jnp.inf, F32)
    i2 = jnp.zeros(in_sel[0].shape, jnp.int32)
    for j in range(EXPERTS_PER_GROUP):
        cand = (i1 != j) & (in_sel[j] > b2)
        i2 = jnp.where(cand, j, i2)
        b2 = jnp.where(cand, in_sel[j], b2)
    p1 = _select(in_prob, i1)
    p2 = _select(in_prob, i2)
    tot = p1 + p2
    experts = (g_idx * EXPERTS_PER_GROUP + i1, g_idx * EXPERTS_PER_GROUP + i2)
    wts_ref[0:1, :] = p1 / tot
    wts_ref[1:2, :] = p2 / tot

    @pl.when(pl.program_id(0) == 0)
    def _():
        count_ref[...] = jnp.zeros_like(count_ref)

    tm = experts[0].shape[1]
    eid = lax.broadcasted_iota(jnp.int32, (N_EXPERTS, tm), 0)
    hit = [eid == e for e in experts]
    chosen = jnp.where(hit[0] | hit[1], 1.0, 0.0)
    before = jnp.dot(chosen.astype(BF16), upper_ref[...], preferred_element_type=F32) + count_ref[...]
    for k in range(TOP_K):
        eidx_ref[k:k + 1, :] = experts[k]
        rank_ref[k:k + 1, :] = jnp.sum(jnp.where(hit[k], before, 0.0), axis=0,
                                       keepdims=True).astype(jnp.int32)
    count_ref[...] += jnp.sum(chosen, axis=1, keepdims=True)
    counts_ref[...] = jnp.broadcast_to(count_ref[...], counts_ref.shape)


def _post_attn(o, x, wo, g_a, ln_g, ln_b, sh_f, sc_f, router_wt, router_b, seq):
    t, d = x.shape
    tm = TM_PROJ
    tiles_per_seq = seq // tm
    mod_spec = pl.BlockSpec((1, 1, d), lambda i: (i // tiles_per_seq, 0, 0))
    vec_spec = pl.BlockSpec((1, d), lambda i: (0, 0))
    tok_spec = pl.BlockSpec((tm, d), lambda i: (i, 0))
    route_spec = pl.BlockSpec((TOP_K, tm), lambda i: (0, i))
    upper = jnp.asarray(np.triu(np.ones((tm, tm), np.float32), k=1), BF16)
    return pl.pallas_call(
        _post_attn_kernel,
        out_shape=(jax.ShapeDtypeStruct((t, d), F32), jax.ShapeDtypeStruct((t * SUBLANES, LANES), F32),
                   jax.ShapeDtypeStruct((TOP_K, t), jnp.int32), jax.ShapeDtypeStruct((TOP_K, t), F32),
                   jax.ShapeDtypeStruct((TOP_K, t), jnp.int32),
                   jax.ShapeDtypeStruct((N_EXPERTS, LANES), F32)),
        grid=(t // tm,),
        in_specs=[
            tok_spec, tok_spec,
            pl.BlockSpec((d, d), lambda i: (0, 0)),
            mod_spec, vec_spec, vec_spec, mod_spec, mod_spec,
            pl.BlockSpec((N_EXPERTS, d), lambda i: (0, 0)),
            pl.BlockSpec((N_EXPERTS, 1), lambda i: (0, 0)),
            pl.BlockSpec((tm, tm), lambda i: (0, 0)),
        ],
        out_specs=(tok_spec, pl.BlockSpec((tm * SUBLANES, LANES), lambda i: (i, 0)),
                   route_spec, route_spec, route_spec,
                   pl.BlockSpec((N_EXPERTS, LANES), lambda i: (0, 0))),
        scratch_shapes=[pltpu.VMEM((N_EXPERTS, 1), F32)],
        compiler_params=_params("arbitrary"),
        name="post_attn",
    )(o, x, wo, g_a, ln_g, ln_b, sh_f, sc_f, router_wt, router_b, upper)


def _routing_tables(eidx, rank, counts, rows_pad):
    t = eidx.shape[1]
    padded = ((counts + TM_EXPERT - 1) // TM_EXPERT) * TM_EXPERT
    ends = jnp.cumsum(padded)
    starts = ends - padded
    start_of = sum(jnp.where(eidx == e, starts[e], 0) for e in range(N_EXPERTS))
    pos = (start_of + rank).reshape(-1)
    n_tiles = rows_pad // TM_EXPERT
    tile_start = jnp.arange(n_tiles, dtype=jnp.int32) * TM_EXPERT
    tile_expert = jnp.minimum(jnp.sum((tile_start[:, None] >= ends[None, :]).astype(jnp.int32), axis=1),
                              N_EXPERTS - 1).astype(jnp.int32)
    n_valid = (ends[-1] // TM_EXPERT).astype(jnp.int32).reshape(1)
    pad_lo = (starts + counts).astype(jnp.int32)
    pad_hi = jnp.concatenate([ends[:-1], jnp.full((1,), rows_pad, ends.dtype)]).astype(jnp.int32)
    return pos.astype(jnp.int32), pad_lo, pad_hi, tile_expert, n_valid


def _row_tile(ref, row8):
    return ref.at[pl.ds(pl.multiple_of(row8, SUBLANES), SUBLANES)]


def _dispatch_kernel(pos8_ref, pad_lo_ref, pad_hi_ref, h_ref, hs_hbm, zero_ref, sem, pad_sem, *, n_tok):
    i = pl.program_id(0)
    tm = h_ref.shape[0] // SUBLANES
    base = i * tm

    def body(r, carry):
        for k in range(TOP_K):
            pltpu.make_async_copy(_row_tile(h_ref, r * SUBLANES),
                                  _row_tile(hs_hbm, pos8_ref[k * n_tok + base + r]), sem).start()
        return carry

    lax.fori_loop(0, tm, body, 0, unroll=8)

    @pl.when(i == 0)
    def _():
        zero_ref[...] = jnp.zeros_like(zero_ref)
        for e in range(N_EXPERTS):
            lo, hi = pad_lo_ref[e], pad_hi_ref[e]

            def start(r, carry):
                pltpu.make_async_copy(zero_ref, _row_tile(hs_hbm, r * SUBLANES), pad_sem).start()
                return carry

            def wait(r, carry):
                pltpu.make_async_copy(zero_ref, _row_tile(hs_hbm, r * SUBLANES), pad_sem).wait()
                return carry

            lax.fori_loop(lo, hi, start, 0)
            lax.fori_loop(lo, hi, wait, 0)

    for k in range(TOP_K):
        pltpu.make_async_copy(h_ref, hs_hbm.at[pl.ds(0, tm * SUBLANES)], sem).wait()


def _dispatch(h_tiled, pos8, pad_lo, pad_hi, rows_pad):
    t = h_tiled.shape[0] // SUBLANES
    tm = TM_DISPATCH
    grid_spec = pltpu.PrefetchScalarGridSpec(
        num_scalar_prefetch=3,
        grid=(t // tm,),
        in_specs=[pl.BlockSpec((tm * SUBLANES, LANES), lambda i, *_: (i, 0))],
        out_specs=pl.BlockSpec(memory_space=pl.ANY),
        scratch_shapes=[pltpu.VMEM((SUBLANES, LANES), F32), pltpu.SemaphoreType.DMA(()),
                        pltpu.SemaphoreType.DMA(())],
    )
    return pl.pallas_call(
        functools.partial(_dispatch_kernel, n_tok=t),
        out_shape=jax.ShapeDtypeStruct((rows_pad * SUBLANES, LANES), F32),
        grid_spec=grid_spec,
        compiler_params=_params("arbitrary"),
        name="dispatch",
    )(pos8, pad_lo, pad_hi, h_tiled)


def _expert_kernel(te_ref, nv_ref, hs_ref, wg_ref, wu_ref, wd_ref, y_ref):
    i = pl.program_id(0)
    nv = nv_ref[0]

    @pl.when(i < nv)
    def _():
        hb = _load_token_tiled(hs_ref, TM_EXPERT).astype(BF16)
        g = jnp.dot(hb, wg_ref[0], preferred_element_type=F32)
        u = jnp.dot(hb, wu_ref[0], preferred_element_type=F32)
        a = g * (1.0 / (1.0 + jnp.exp(-g))) * u
        _store_token_tiled(y_ref, jnp.dot(a.astype(BF16), wd_ref[0], preferred_element_type=F32))

    @pl.when(i >= nv)
    def _():
        y_ref[...] = jnp.zeros_like(y_ref)


def _experts(hs, tile_expert, n_valid, wg, wu, wd):
    d = wg.shape[1]
    rows_pad = hs.shape[0] // SUBLANES
    n_tiles = rows_pad // TM_EXPERT
    grid_spec = pltpu.PrefetchScalarGridSpec(
        num_scalar_prefetch=2,
        grid=(n_tiles,),
        in_specs=[
            pl.BlockSpec((TM_EXPERT * SUBLANES, LANES), lambda i, te, nv: (i, 0)),
            pl.BlockSpec((1, d, D_EXPERT), lambda i, te, nv: (te[i], 0, 0)),
            pl.BlockSpec((1, d, D_EXPERT), lambda i, te, nv: (te[i], 0, 0)),
            pl.BlockSpec((1, D_EXPERT, d), lambda i, te, nv: (te[i], 0, 0)),
        ],
        out_specs=pl.BlockSpec((TM_EXPERT * SUBLANES, LANES), lambda i, te, nv: (i, 0)),
    )
    return pl.pallas_call(
        _expert_kernel,
        out_shape=jax.ShapeDtypeStruct((rows_pad * SUBLANES, LANES), F32),
        grid_spec=grid_spec,
        compiler_params=_params("arbitrary"),
        name="experts",
    )(tile_expert, n_valid, hs, wg, wu, wd)


def _combine_kernel(pos_ref, y_hbm, x_ref, w_ref, gf_ref, lng_ref, lnb_ref, o_ref, buf, sem, *, n_tok):
    i = pl.program_id(0)
    n = pl.num_programs(0)
    slot = i % 2
    tm = x_ref.shape[0]

    def issue(tile, slot_):
        base = tile * tm

        def body(r, carry):
            for k in range(TOP_K):
                pltpu.make_async_copy(_row_tile(y_hbm, pos_ref[k * n_tok + base + r]),
                                      _row_tile(buf.at[slot_, k], r * SUBLANES), sem.at[slot_]).start()
            return carry

        lax.fori_loop(0, tm, body, 0, unroll=8)

    @pl.when(i == 0)
    def _():
        issue(0, 0)

    @pl.when(i + 1 < n)
    def _():
        issue(i + 1, 1 - slot)

    for k in range(TOP_K):
        pltpu.make_async_copy(y_hbm.at[pl.ds(0, tm * SUBLANES)], buf.at[slot, k], sem.at[slot]).wait()
    w = w_ref[...]
    m = (w[:, 0:1] * _load_token_tiled(buf.at[slot, 0], tm)
         + w[:, 1:2] * _load_token_tiled(buf.at[slot, 1], tm))
    z = DEEPNORM_ALPHA * x_ref[...] + (1.0 + gf_ref[0]) * m
    o_ref[...] = _layer_norm(z, lng_ref[...], lnb_ref[...])


def _combine(pos, y, x, wts_t, g_f, ln_g, ln_b, seq):
    t, d = x.shape
    tm = TM_COMBINE
    tiles_per_seq = seq // tm
    grid_spec = pltpu.PrefetchScalarGridSpec(
        num_scalar_prefetch=1,
        grid=(t // tm,),
        in_specs=[
            pl.BlockSpec(memory_space=pl.ANY),
            pl.BlockSpec((tm, d), lambda i, pos_: (i, 0)),
            pl.BlockSpec((tm, TOP_K), lambda i, pos_: (i, 0)),
            pl.BlockSpec((1, 1, d), lambda i, pos_: (i // tiles_per_seq, 0, 0)),
            pl.BlockSpec((1, d), lambda i, pos_: (0, 0)),
            pl.BlockSpec((1, d), lambda i, pos_: (0, 0)),
        ],
        out_specs=pl.BlockSpec((tm, d), lambda i, pos_: (i, 0)),
        scratch_shapes=[pltpu.VMEM((2, TOP_K, tm * SUBLANES, LANES), F32),
                        pltpu.SemaphoreType.DMA((2,))],
    )
    return pl.pallas_call(
        functools.partial(_combine_kernel, n_tok=t),
        out_shape=jax.ShapeDtypeStruct((t, d), F32),
        grid_spec=grid_spec,
        compiler_params=_params("arbitrary"),
        name="combine",
    )(pos, y, x, wts_t, g_f, ln_g, ln_b)


def kernel(x, c, ada_w, ada_b, ln_attn_g, ln_attn_b, ln_ffn_g, ln_ffn_b, fox_w_in, fox_b_f, fox_w_o,
           kv_ada_w, kv_ada_b, kv_w, diff_w_q, diff_lambda, diff_norm_g, diff_w_o, router_w, router_b,
           moe_w_gate, moe_w_up, moe_w_down):
    bsz, seq, d = x.shape
    t = bsz * seq
    rows_pad = TOP_K * t + N_EXPERTS * TM_EXPERT
    nq = seq // TQ

    mod = _ada(c, ada_w, ada_b, 1536)
    kv_mod = _ada(c, kv_ada_w[None], kv_ada_b[None], 1024)[0]

    def mod_part(l, k):
        return mod[l, :, k * d:(k + 1) * d].reshape(bsz, 1, d)

    router_wt = router_w.T
    router_bc = router_b.reshape(N_EXPERTS, 1)
    xt = x.reshape(t, d)

    k_sh = v_sh_t = None
    alibi_kb = _alibi_key_bias(seq)
    for l in range(DEPTH):
        sh_a, sc_a, g_a, sh_f, sc_f, g_f = (mod_part(l, k) for k in range(6))
        if l < N_A:
            w_in = fox_w_in[l]
            w_qt = w_in[:, :d].T.astype(BF16)
            w_k = w_in[:, d:2 * d].astype(BF16)
            w_vt = w_in[:, 2 * d:3 * d].T.astype(BF16)
            w_f = jnp.pad(w_in[:, 3 * d:], ((0, 0), (0, HEAD_BLOCK - FOX_HEADS))).astype(BF16)
            b_f = jnp.pad(fox_b_f[l], (0, HEAD_BLOCK - FOX_HEADS)).reshape(1, HEAD_BLOCK)
            k_arr, q_t, v_t, key_bias, cum_t = _modproj(xt, sh_a, sc_a, [w_k], [w_qt, w_vt], seq,
                                                        forget=(w_f, b_f))
            o = _attention(
                "fox", q_t, k_arr, 0, key_bias,
                pl.BlockSpec((seq, HEAD_BLOCK), lambda b, h, i: (b, h)),
                v_t, 0, (cum_t.reshape(N_HEAD_BLOCKS, 2, t),),
                [pl.BlockSpec((1, 2, TQ), lambda b, h, i: (h, 0, b * nq + i))],
                bsz, seq)
            w_o = fox_w_o[l].astype(BF16)
        else:
            j = l - N_A
            if l == N_A:
                k_sh, v_sh_t = _modproj(xt, kv_mod[:, :d].reshape(bsz, 1, d),
                                        kv_mod[:, d:].reshape(bsz, 1, d),
                                        [kv_w[:, :d].astype(BF16)], [kv_w[:, d:].T.astype(BF16)], seq)
            (q_t,) = _modproj(xt, sh_a, sc_a, [], [diff_w_q[j].T.astype(BF16)], seq)
            lam_init = 0.8 - 0.6 * math.exp(-0.3 * l)
            o = _attention(
                "diff", q_t, k_sh, 0, alibi_kb,
                pl.BlockSpec((seq, HEAD_BLOCK), lambda b, h, i: (0, 0)),
                v_sh_t, 0,
                (diff_lambda[j][None], diff_norm_g[j].reshape(1, HEAD_BLOCK)),
                [pl.BlockSpec((1, 4, HALF), lambda b, h, i: (0, 0, 0)),
                 pl.BlockSpec((1, HEAD_BLOCK), lambda b, h, i: (0, 0))],
                bsz, seq, lam_init=lam_init)
            w_o = diff_w_o[j].astype(BF16)

        xt, h, eidx, wts, rank, counts = _post_attn(
            o, xt, w_o, g_a, ln_attn_g[l].reshape(1, d), ln_attn_b[l].reshape(1, d), sh_f, sc_f,
            router_wt, router_bc, seq)
        pos, pad_lo, pad_hi, tile_expert, n_valid = _routing_tables(
            eidx, rank, counts[:, 0].astype(jnp.int32), rows_pad)
        pos8 = pos * SUBLANES
        hs = _dispatch(h, pos8, pad_lo, pad_hi, rows_pad)
        y = _experts(hs, tile_expert, n_valid, moe_w_gate[l].astype(BF16),
                     moe_w_up[l].astype(BF16), moe_w_down[l].astype(BF16))
        xt = _combine(pos8, y, xt, wts.T, g_f, ln_ffn_g[l].reshape(1, d),
                      ln_ffn_b[l].reshape(1, d), seq)
    return xt.reshape(bsz, seq, d)
```

```python
import functools
import math

import jax
import jax.numpy as jnp
import numpy as np
from jax import lax
from jax.experimental import pallas as pl
from jax.experimental.pallas import tpu as pltpu

F32 = jnp.float32
BF16 = jnp.bfloat16
HIGHEST = lax.Precision.HIGHEST

D_MODEL = 1024
DEPTH = 4
N_A = DEPTH // 2
HEAD_BLOCK = 128
N_HEAD_BLOCKS = D_MODEL // HEAD_BLOCK
HALF = HEAD_BLOCK // 2
FOX_HEADS = 16
DIFF_HEADS = 8
N_EXPERTS = 16
N_GROUPS = 4
EXPERTS_PER_GROUP = N_EXPERTS // N_GROUPS
TOP_K = 2
D_EXPERT = 512
DEEPNORM_ALPHA = (2 * DEPTH) ** 0.25
LN_EPS = 1e-5
NEG_INF = -1e30
LOG2E = math.log2(math.e)

VMEM_LIMIT = 56 * 1024 * 1024

TM_PROJ = 512
TN_PROJ = 256
TQ = 512
TK = 512
DIAG_STRIPS = 2
BLOCKS_PER_TRIP = 4
LOOKAHEAD = 3
BIAS_ROWS = 16
ONES_ROWS = 16
TM_EXPERT = 512
TM_COMBINE = 256
TM_DISPATCH = 1024

NT_DIMS = (((1,), (1,)), ((), ()))


def _params(*sem, flags=None):
    return pltpu.CompilerParams(dimension_semantics=sem, vmem_limit_bytes=VMEM_LIMIT, flags=flags)


def _split3(v):
    hi = v.astype(BF16).astype(F32)
    r = v - hi
    mid = r.astype(BF16).astype(F32)
    lo = (r - mid).astype(BF16).astype(F32)
    return hi, mid, lo


def _ada_kernel(c_ref, w_ref, b_ref, o_ref):
    c = c_ref[...]
    sc = c * (1.0 / (1.0 + jnp.exp(-c)))
    o_ref[0] = jnp.dot(sc, w_ref[0], precision=HIGHEST, preferred_element_type=F32) + b_ref[0]


def _ada(c, w, b, tn):
    n_layers, d, n = w.shape
    bsz = c.shape[0]
    return pl.pallas_call(
        _ada_kernel,
        out_shape=jax.ShapeDtypeStruct((n_layers, bsz, n), F32),
        grid=(n_layers, n // tn),
        in_specs=[
            pl.BlockSpec((bsz, d), lambda l, j: (0, 0)),
            pl.BlockSpec((1, d, tn), lambda l, j: (l, 0, j)),
            pl.BlockSpec((1, 1, tn), lambda l, j: (l, 0, j)),
        ],
        out_specs=pl.BlockSpec((1, bsz, tn), lambda l, j: (l, 0, j)),
        compiler_params=_params("parallel", "parallel"),
        name="ada_params",
    )(c, w, b.reshape(n_layers, 1, n))


def _modproj_kernel(*refs, n_std, n_t, with_forget, tiles_per_seq):
    x_ref, sh_ref, sc_ref = refs[:3]
    pos = 3
    wstd = refs[pos:pos + n_std]; pos += n_std
    wt = refs[pos:pos + n_t]; pos += n_t
    if with_forget:
        wf_ref, bf_ref, sel_ref, ones_ref = refs[pos:pos + 4]; pos += 4
    ostd = refs[pos:pos + n_std]; pos += n_std
    ot = refs[pos:pos + n_t]; pos += n_t
    if with_forget:
        kb_ref, cumt_ref, carry_ref = refs[pos:pos + 3]

    hb = (x_ref[...] * (1.0 + sc_ref[0]) + sh_ref[0]).astype(BF16)
    for w_ref, o_ref in zip(wstd, ostd):
        for j in range(w_ref.shape[1] // TN_PROJ):
            cols = slice(j * TN_PROJ, (j + 1) * TN_PROJ)
            o_ref[:, cols] = jnp.dot(hb, w_ref[:, cols], preferred_element_type=F32).astype(o_ref.dtype)
    for w_ref, o_ref in zip(wt, ot):
        for j in range(w_ref.shape[0] // TN_PROJ):
            rows = slice(j * TN_PROJ, (j + 1) * TN_PROJ)
            o_ref[rows, :] = lax.dot_general(w_ref[rows, :], hb, NT_DIMS,
                                             preferred_element_type=F32).astype(o_ref.dtype)

    if with_forget:
        @pl.when(pl.program_id(0) % tiles_per_seq == 0)
        def _():
            carry_ref[...] = jnp.zeros_like(carry_ref)

        z = jnp.dot(hb, wf_ref[...], preferred_element_type=F32) + bf_ref[...]
        logf = jnp.minimum(z, 0.0) - jnp.log(1.0 + jnp.exp(-jnp.abs(z)))
        tm = z.shape[0]
        lower = (lax.broadcasted_iota(jnp.int32, (tm, tm), 0)
                 >= lax.broadcasted_iota(jnp.int32, (tm, tm), 1)).astype(F32)
        cum = jnp.dot(lower, logf, precision=HIGHEST, preferred_element_type=F32) + carry_ref[...]
        cumt_ref[...] = cum.T[:FOX_HEADS, :]
        carry_ref[...] = cum[tm - 1:tm, :]
        terms = jnp.concatenate([v.astype(BF16) for v in _split3((-LOG2E) * cum)], axis=1)
        for j in range(sel_ref.shape[1] // TN_PROJ):
            cols = slice(j * TN_PROJ, (j + 1) * TN_PROJ)
            kb_ref[:, cols] = (jnp.dot(terms, sel_ref[:, cols], preferred_element_type=F32)
                               + ones_ref[:, cols]).astype(kb_ref.dtype)


def _fox_bias_selection():
    sel = np.zeros((3 * HEAD_BLOCK, D_MODEL), np.float32)
    ones = np.zeros((1, D_MODEL), np.float32)
    for hp in range(N_HEAD_BLOCKS):
        for a in range(2):
            for term in range(3):
                sel[term * HEAD_BLOCK + 2 * hp + a, hp * HEAD_BLOCK + 6 * a + term] = 1.0
            ones[0, hp * HEAD_BLOCK + 6 * a + 3:hp * HEAD_BLOCK + 6 * a + 6] = 1.0
    return jnp.asarray(sel, BF16), jnp.asarray(ones, F32)


def _modproj(x, shift, scale, w_std, w_t, seq, forget=None):
    t, d = x.shape
    tm = TM_PROJ
    tiles_per_seq = seq // tm
    const = lambda i: (0, 0)
    mod_spec = pl.BlockSpec((1, 1, d), lambda i: (i // tiles_per_seq, 0, 0))
    in_specs = [pl.BlockSpec((tm, d), lambda i: (i, 0)), mod_spec, mod_spec]
    in_specs += [pl.BlockSpec(w.shape, const) for w in w_std]
    in_specs += [pl.BlockSpec(w.shape, const) for w in w_t]
    out_shape = [jax.ShapeDtypeStruct((t, w.shape[1]), BF16) for w in w_std]
    out_shape += [jax.ShapeDtypeStruct((w.shape[0], t), BF16) for w in w_t]
    out_specs = [pl.BlockSpec((tm, w.shape[1]), lambda i: (i, 0)) for w in w_std]
    out_specs += [pl.BlockSpec((w.shape[0], tm), lambda i: (0, i)) for w in w_t]
    args = [x, shift, scale, *w_std, *w_t]
    scratch = []
    if forget is not None:
        wf, bf = forget
        sel, ones = _fox_bias_selection()
        in_specs += [pl.BlockSpec(a.shape, const) for a in (wf, bf, sel, ones)]
        args += [wf, bf, sel, ones]
        out_shape += [jax.ShapeDtypeStruct((t, d), BF16), jax.ShapeDtypeStruct((FOX_HEADS, t), F32)]
        out_specs += [pl.BlockSpec((tm, d), lambda i: (i, 0)),
                      pl.BlockSpec((FOX_HEADS, tm), lambda i: (0, i))]
        scratch = [pltpu.VMEM((1, HEAD_BLOCK), F32)]
    return pl.pallas_call(
        functools.partial(_modproj_kernel, n_std=len(w_std), n_t=len(w_t),
                          with_forget=forget is not None, tiles_per_seq=tiles_per_seq),
        out_shape=tuple(out_shape),
        grid=(t // tm,),
        in_specs=in_specs,
        out_specs=tuple(out_specs),
        scratch_shapes=scratch,
        compiler_params=_params("arbitrary"),
        name="modproj_fox" if forget is not None else "modproj",
    )(*args)


def _bias_rows(entries, tq):
    rid = lax.broadcasted_iota(jnp.int32, (BIAS_ROWS, tq), 0)
    out = jnp.zeros((BIAS_ROWS, tq), F32)
    for r, v in entries.items():
        out = jnp.where(rid == r, v, out)
    return out


def _attn_kernel(*refs, mode, lam_init, n_heads):
    if mode == "fox":
        q_ref, k_ref, kb_ref, v_ref, cumt_ref, o_ref, m_ref, acc_ref = refs
    else:
        q_ref, k_ref, kb_ref, v_ref, lam_ref, g_ref, o_ref, m_ref, acc_ref = refs
    hp = pl.program_id(1)
    i = pl.program_id(2)
    tq = q_ref.shape[1]

    qs = q_ref[...] * jnp.asarray(HALF ** -0.5 * LOG2E, BF16)
    zeros_half = jnp.zeros((HALF, tq), BF16)
    q_tops = (jnp.concatenate([qs[:HALF], zeros_half], axis=0),
              jnp.concatenate([zeros_half, qs[HALF:]], axis=0))
    pad = jnp.zeros((HEAD_BLOCK - BIAS_ROWS, tq), BF16)
    if mode == "fox":
        q_bias = []
        for a in range(2):
            c = _split3(cumt_ref[0, a:a + 1, :] * LOG2E)
            base = 6 * a
            rows = {base: 1.0, base + 1: 1.0, base + 2: 1.0,
                    base + 3: c[0], base + 4: c[1], base + 5: c[2]}
            q_bias.append(_bias_rows(rows, tq).astype(BF16))
    else:
        slope = jnp.exp2(jnp.full((1, 1), -8.0, F32) * (hp + 1).astype(F32) / n_heads)
        l3 = _split3(slope * LOG2E)
        qpos = (i * tq + lax.broadcasted_iota(jnp.int32, (1, tq), 1)).astype(F32)
        t3 = _split3(-(slope * LOG2E) * qpos)
        rows = {0: l3[0], 1: l3[0], 2: l3[1], 3: l3[1], 4: l3[2], 5: l3[2],
                6: t3[0], 7: t3[1], 8: t3[2]}
        q_bias = [_bias_rows(rows, tq).astype(BF16)] * 2
    q_aug = tuple(jnp.concatenate([q_tops[a], q_bias[a], pad], axis=0) for a in range(2))

    m_ref[...] = jnp.full_like(m_ref, NEG_INF)
    acc_ref[...] = jnp.zeros_like(acc_ref)
    ones_rows = jnp.ones((ONES_ROWS, TK), BF16)

    def key_slice(k0, nk):
        return pl.ds(pl.multiple_of(k0, nk), nk)

    def scores(a, k0, nk, q0, nq):
        ks = key_slice(k0, nk)
        k_aug = jnp.concatenate([k_ref[ks, :], kb_ref[ks, :]], axis=1)
        return jnp.dot(k_aug, q_aug[a][:, q0:q0 + nq], preferred_element_type=F32)

    v_rows = [slice(a * HALF, (a + 1) * HALF) if mode == "fox" else slice(0, HEAD_BLOCK)
              for a in range(2)]
    n_v = v_rows[0].stop - v_rows[0].start

    def accumulate(a, k0, nk, q0, nq, masked, s):
        ql = slice(q0, q0 + nq)
        v_aug = jnp.concatenate([v_ref[v_rows[a], key_slice(k0, nk)], ones_rows[:, :nk]], axis=0)
        if masked:
            keep = ((k0 + lax.broadcasted_iota(jnp.int32, s.shape, 0))
                    <= (i * tq + q0 + lax.broadcasted_iota(jnp.int32, s.shape, 1)))
            s = jnp.where(keep, s, NEG_INF)
        m_old = m_ref[a, :, ql]
        m_new = jnp.maximum(m_old, jnp.max(s, axis=0, keepdims=True))
        alpha = jnp.exp2(m_old - m_new)
        p = jnp.exp2(s - m_new).astype(BF16)
        acc_ref[a, :, ql] = alpha * acc_ref[a, :, ql] + jnp.dot(v_aug, p, preferred_element_type=F32)
        m_ref[a, :, ql] = m_new

    def full_block(j):
        return [(a, j * TK, TK, 0, tq, False) for a in range(2)]

    def diagonal_block(j):
        strip = TK // DIAG_STRIPS
        return [(a, j * TK + c * strip, strip, c * strip, tq - c * strip, True)
                for c in range(DIAG_STRIPS) for a in range(2)]

    def process(work):
        pending = [scores(*w[:5]) for w in work[:LOOKAHEAD]]
        for n, w in enumerate(work):
            s = pending.pop(0)
            if n + LOOKAHEAD < len(work):
                pending.append(scores(*work[n + LOOKAHEAD][:5]))
            accumulate(*w, s)

    n_full = (i * tq) // TK

    def loop_body(p, carry):
        process(sum((full_block(BLOCKS_PER_TRIP * p + d) for d in range(BLOCKS_PER_TRIP)), []))
        return carry

    lax.fori_loop(0, n_full // BLOCKS_PER_TRIP, loop_body, 0)

    for rem in range(BLOCKS_PER_TRIP):
        @pl.when(n_full % BLOCKS_PER_TRIP == rem)
        def _(rem=rem):
            process(sum((full_block(n_full - rem + d) for d in range(rem)), [])
                    + diagonal_block(n_full))

    inv = [1.0 / acc_ref[a, n_v:n_v + 1, :] for a in range(2)]
    if mode == "fox":
        o_t = jnp.concatenate([acc_ref[a, :n_v, :] * inv[a] for a in range(2)], axis=0)
        o_ref[...] = o_t.T.astype(o_ref.dtype)
    else:
        lp = lam_ref[0]
        lam = (jnp.exp(jnp.sum(lp[0:1] * lp[1:2], axis=-1, keepdims=True))
               - jnp.exp(jnp.sum(lp[2:3] * lp[3:4], axis=-1, keepdims=True)) + lam_init)
        o_t = acc_ref[0, :HEAD_BLOCK, :] * inv[0] - lam * (acc_ref[1, :HEAD_BLOCK, :] * inv[1])
        o = o_t.T
        o = o * lax.rsqrt(jnp.mean(o * o, axis=-1, keepdims=True) + LN_EPS)
        o_ref[...] = (o * g_ref[...] * (1.0 - lam_init)).astype(o_ref.dtype)


def _attention(mode, q_t, k_arr, k_off, kb, kb_spec, v_t, v_off, extras, extra_specs, batch, seq,
               lam_init=0.0, n_heads=DIFF_HEADS):
    assert TQ == TK, "one masked diagonal block per query tile"
    nq = seq // TQ
    t = batch * seq
    acc_rows = (HALF if mode == "fox" else HEAD_BLOCK) + ONES_ROWS
    in_specs = [
        pl.BlockSpec((HEAD_BLOCK, TQ), lambda b, h, i: (h, b * nq + i)),
        pl.BlockSpec((seq, HEAD_BLOCK), lambda b, h, i: (b, k_off + h)),
        kb_spec,
        pl.BlockSpec((HEAD_BLOCK, seq), lambda b, h, i: (v_off + h, b)),
    ] + extra_specs
    return pl.pallas_call(
        functools.partial(_attn_kernel, mode=mode, lam_init=lam_init, n_heads=n_heads),
        out_shape=jax.ShapeDtypeStruct((t, D_MODEL), BF16),
        grid=(batch, N_HEAD_BLOCKS, nq),
        in_specs=in_specs,
        out_specs=pl.BlockSpec((TQ, HEAD_BLOCK), lambda b, h, i: (b * nq + i, h)),
        scratch_shapes=[pltpu.VMEM((2, 1, TQ), F32),
                        pltpu.VMEM((2, acc_rows, TQ), F32)],
        compiler_params=_params("parallel", "parallel", "arbitrary"),
        name="attn_" + mode,
    )(q_t, k_arr, kb, v_t, *extras)


def _bias_lanes(cols):
    lanes = jnp.stack(cols, axis=-1)
    pad = [(0, 0)] * (lanes.ndim - 1) + [(0, HEAD_BLOCK - lanes.shape[-1])]
    return jnp.pad(lanes, pad).astype(BF16)


def _alibi_key_bias(seq):
    kpos = jnp.arange(seq, dtype=jnp.int32)
    hi = ((kpos // 64) * 64).astype(F32)
    lo = (kpos % 64).astype(F32)
    one = jnp.ones((seq,), F32)
    return _bias_lanes([hi, lo, hi, lo, hi, lo, one, one, one])


SUBLANES = 8
LANES = 128
N_DMA_PRIORITIES = 2


def _store_token_tiled(ref, value):
    n = value.shape[0]
    for s in range(SUBLANES):
        ref[pl.ds(s, n, stride=SUBLANES), :] = value[:, s * LANES:(s + 1) * LANES]


def _load_token_tiled(ref, n):
    return jnp.concatenate([ref[pl.ds(s, n, stride=SUBLANES), :] for s in range(SUBLANES)], axis=1)


def _layer_norm(z, g, b):
    mu = jnp.mean(z, axis=-1, keepdims=True)
    zc = z - mu
    var = jnp.mean(zc * zc, axis=-1, keepdims=True)
    return zc * lax.rsqrt(var + LN_EPS) * g + b


def _argmax_first(vals):
    best, idx = vals[0], jnp.zeros(vals[0].shape, jnp.int32)
    for j in range(1, len(vals)):
        better = vals[j] > best
        idx = jnp.where(better, j, idx)
        best = jnp.where(better, vals[j], best)
    return idx


def _select(vals, idx):
    out = vals[0]
    for j in range(1, len(vals)):
        out = jnp.where(idx == j, vals[j], out)
    return out


def _post_attn_kernel(o_ref, x_ref, wo_ref, ga_ref, lng_ref, lnb_ref, shf_ref, scf_ref,
                      rwt_ref, rb_ref, upper_ref, xo_ref, h_ref, eidx_ref, wts_ref, rank_ref,
                      counts_ref, count_ref):
    y = jnp.dot(o_ref[...], wo_ref[...], preferred_element_type=F32)
    z = DEEPNORM_ALPHA * x_ref[...] + (1.0 + ga_ref[0]) * y
    xn = _layer_norm(z, lng_ref[...], lnb_ref[...])
    xo_ref[...] = xn
    h = xn * (1.0 + scf_ref[0]) + shf_ref[0]
    _store_token_tiled(h_ref, h)

    logits = lax.dot_general(rwt_ref[...], h, NT_DIMS, precision=HIGHEST,
                             preferred_element_type=F32)
    rows = [logits[e:e + 1, :] for e in range(N_EXPERTS)]
    mx = functools.reduce(jnp.maximum, rows)
    ex = [jnp.exp(r - mx) for r in rows]
    den = functools.reduce(lambda a, b: a + b, ex)
    probs = [e_ / den for e_ in ex]
    sel = [probs[e] + rb_ref[e:e + 1, :] for e in range(N_EXPERTS)]

    grp_scores = []
    for g in range(N_GROUPS):
        v = sel[g * EXPERTS_PER_GROUP:(g + 1) * EXPERTS_PER_GROUP]
        pair_sums = [v[a] + v[b] for a in range(EXPERTS_PER_GROUP) for b in range(a + 1, EXPERTS_PER_GROUP)]
        grp_scores.append(functools.reduce(jnp.maximum, pair_sums))
    g_idx = _argmax_first(grp_scores)

    in_sel = [_select([sel[g * EXPERTS_PER_GROUP + j] for g in range(N_GROUPS)], g_idx)
              for j in range(EXPERTS_PER_GROUP)]
    in_prob = [_select([probs[g * EXPERTS_PER_GROUP + j] for g in range(N_GROUPS)], g_idx)
               for j in range(EXPERTS_PER_GROUP)]
    i1 = _argmax_first(in_sel)
    b2 = jnp.full(in_sel[0].shape, -jnp.inf, F32)
    i2 = jnp.zeros(in_sel[0].shape, jnp.int32)
    for j in range(EXPERTS_PER_GROUP):
        cand = (i1 != j) & (in_sel[j] > b2)
        i2 = jnp.where(cand, j, i2)
        b2 = jnp.where(cand, in_sel[j], b2)
    p1 = _select(in_prob, i1)
    p2 = _select(in_prob, i2)
    tot = p1 + p2
    experts = (g_idx * EXPERTS_PER_GROUP + i1, g_idx * EXPERTS_PER_GROUP + i2)
    wts_ref[0:1, :] = p1 / tot
    wts_ref[1:2, :] = p2 / tot

    @pl.when(pl.program_id(0) == 0)
    def _():
        count_ref[...] = jnp.zeros_like(count_ref)

    tm = experts[0].shape[1]
    eid = lax.broadcasted_iota(jnp.int32, (N_EXPERTS, tm), 0)
    hit = [eid == e for e in experts]
    chosen = jnp.where(hit[0] | hit[1], 1.0, 0.0)
    before = jnp.dot(chosen.astype(BF16), upper_ref[...], preferred_element_type=F32) + count_ref[...]
    for k in range(TOP_K):
        eidx_ref[k:k + 1, :] = experts[k]
        rank_ref[k:k + 1, :] = jnp.sum(jnp.where(hit[k], before, 0.0), axis=0,
                                       keepdims=True).astype(jnp.int32)
    count_ref[...] += jnp.sum(chosen, axis=1, keepdims=True)
    counts_ref[...] = jnp.broadcast_to(count_ref[...], counts_ref.shape)


def _post_attn(o, x, wo, g_a, ln_g, ln_b, sh_f, sc_f, router_wt, router_b, seq):
    t, d = x.shape
    tm = TM_PROJ
    tiles_per_seq = seq // tm
    mod_spec = pl.BlockSpec((1, 1, d), lambda i: (i // tiles_per_seq, 0, 0))
    vec_spec = pl.BlockSpec((1, d), lambda i: (0, 0))
    tok_spec = pl.BlockSpec((tm, d), lambda i: (i, 0))
    route_spec = pl.BlockSpec((TOP_K, tm), lambda i: (0, i))
    upper = jnp.asarray(np.triu(np.ones((tm, tm), np.float32), k=1), BF16)
    return pl.pallas_call(
        _post_attn_kernel,
        out_shape=(jax.ShapeDtypeStruct((t, d), F32), jax.ShapeDtypeStruct((t * SUBLANES, LANES), F32),
                   jax.ShapeDtypeStruct((TOP_K, t), jnp.int32), jax.ShapeDtypeStruct((TOP_K, t), F32),
                   jax.ShapeDtypeStruct((TOP_K, t), jnp.int32),
                   jax.ShapeDtypeStruct((N_EXPERTS, LANES), F32)),
        grid=(t // tm,),
        in_specs=[
            tok_spec, tok_spec,
            pl.BlockSpec((d, d), lambda i: (0, 0)),
            mod_spec, vec_spec, vec_spec, mod_spec, mod_spec,
            pl.BlockSpec((N_EXPERTS, d), lambda i: (0, 0)),
            pl.BlockSpec((N_EXPERTS, 1), lambda i: (0, 0)),
            pl.BlockSpec((tm, tm), lambda i: (0, 0)),
        ],
        out_specs=(tok_spec, pl.BlockSpec((tm * SUBLANES, LANES), lambda i: (i, 0)),
                   route_spec, route_spec, route_spec,
                   pl.BlockSpec((N_EXPERTS, LANES), lambda i: (0, 0))),
        scratch_shapes=[pltpu.VMEM((N_EXPERTS, 1), F32)],
        compiler_params=_params("arbitrary"),
        name="post_attn",
    )(o, x, wo, g_a, ln_g, ln_b, sh_f, sc_f, router_wt, router_b, upper)


def _routing_tables(eidx, rank, counts, rows_pad):
    t = eidx.shape[1]
    padded = ((counts + TM_EXPERT - 1) // TM_EXPERT) * TM_EXPERT
    ends = jnp.cumsum(padded)
    starts = ends - padded
    start_of = sum(jnp.where(eidx == e, starts[e], 0) for e in range(N_EXPERTS))
    pos = (start_of + rank).reshape(-1)
    n_tiles = rows_pad // TM_EXPERT
    tile_start = jnp.arange(n_tiles, dtype=jnp.int32) * TM_EXPERT
    tile_expert = jnp.minimum(jnp.sum((tile_start[:, None] >= ends[None, :]).astype(jnp.int32), axis=1),
                              N_EXPERTS - 1).astype(jnp.int32)
    n_valid = (ends[-1] // TM_EXPERT).astype(jnp.int32).reshape(1)
    pad_lo = (starts + counts).astype(jnp.int32)
    pad_hi = jnp.concatenate([ends[:-1], jnp.full((1,), rows_pad, ends.dtype)]).astype(jnp.int32)
    return pos.astype(jnp.int32), pad_lo, pad_hi, tile_expert, n_valid


def _row_tile(ref, row8):
    return ref.at[pl.ds(pl.multiple_of(row8, SUBLANES), SUBLANES)]


def _dispatch_kernel(pos8_ref, pad_lo_ref, pad_hi_ref, h_ref, hs_hbm, zero_ref, sem, pad_sem, *, n_tok):
    i = pl.program_id(0)
    tm = h_ref.shape[0] // SUBLANES
    base = i * tm

    def body(r, carry):
        for k in range(TOP_K):
            pltpu.make_async_copy(_row_tile(h_ref, r * SUBLANES),
                                  _row_tile(hs_hbm, pos8_ref[k * n_tok + base + r]),
                                  sem).start(priority=k % N_DMA_PRIORITIES)
        return carry

    lax.fori_loop(0, tm, body, 0, unroll=8)

    @pl.when(i == 0)
    def _():
        zero_ref[...] = jnp.zeros_like(zero_ref)
        for e in range(N_EXPERTS):
            lo, hi = pad_lo_ref[e], pad_hi_ref[e]

            def start(r, carry):
                pltpu.make_async_copy(zero_ref, _row_tile(hs_hbm, r * SUBLANES), pad_sem).start()
                return carry

            def wait(r, carry):
                pltpu.make_async_copy(zero_ref, _row_tile(hs_hbm, r * SUBLANES), pad_sem).wait()
                return carry

            lax.fori_loop(lo, hi, start, 0)
            lax.fori_loop(lo, hi, wait, 0)

    for k in range(TOP_K):
        pltpu.make_async_copy(h_ref, hs_hbm.at[pl.ds(0, tm * SUBLANES)], sem).wait()


def _dispatch(h_tiled, pos8, pad_lo, pad_hi, rows_pad):
    t = h_tiled.shape[0] // SUBLANES
    tm = TM_DISPATCH
    grid_spec = pltpu.PrefetchScalarGridSpec(
        num_scalar_prefetch=3,
        grid=(t // tm,),
        in_specs=[pl.BlockSpec((tm * SUBLANES, LANES), lambda i, *_: (i, 0))],
        out_specs=pl.BlockSpec(memory_space=pl.ANY),
        scratch_shapes=[pltpu.VMEM((SUBLANES, LANES), F32), pltpu.SemaphoreType.DMA(()),
                        pltpu.SemaphoreType.DMA(())],
    )
    return pl.pallas_call(
        functools.partial(_dispatch_kernel, n_tok=t),
        out_shape=jax.ShapeDtypeStruct((rows_pad * SUBLANES, LANES), F32),
        grid_spec=grid_spec,
        compiler_params=_params("arbitrary"),
        name="dispatch",
    )(pos8, pad_lo, pad_hi, h_tiled)


def _expert_kernel(te_ref, nv_ref, hs_ref, wg_ref, wu_ref, wd_ref, y_ref):
    i = pl.program_id(0)
    nv = nv_ref[0]

    @pl.when(i < nv)
    def _():
        hb = _load_token_tiled(hs_ref, TM_EXPERT).astype(BF16)
        g = jnp.dot(hb, wg_ref[0], preferred_element_type=F32)
        u = jnp.dot(hb, wu_ref[0], preferred_element_type=F32)
        a = g * (1.0 / (1.0 + jnp.exp(-g))) * u
        _store_token_tiled(y_ref, jnp.dot(a.astype(BF16), wd_ref[0], preferred_element_type=F32))

    @pl.when(i >= nv)
    def _():
        y_ref[...] = jnp.zeros_like(y_ref)


def _experts(hs, tile_expert, n_valid, wg, wu, wd):
    d = wg.shape[1]
    rows_pad = hs.shape[0] // SUBLANES
    n_tiles = rows_pad // TM_EXPERT
    grid_spec = pltpu.PrefetchScalarGridSpec(
        num_scalar_prefetch=2,
        grid=(n_tiles,),
        in_specs=[
            pl.BlockSpec((TM_EXPERT * SUBLANES, LANES), lambda i, te, nv: (i, 0)),
            pl.BlockSpec((1, d, D_EXPERT), lambda i, te, nv: (te[i], 0, 0)),
            pl.BlockSpec((1, d, D_EXPERT), lambda i, te, nv: (te[i], 0, 0)),
            pl.BlockSpec((1, D_EXPERT, d), lambda i, te, nv: (te[i], 0, 0)),
        ],
        out_specs=pl.BlockSpec((TM_EXPERT * SUBLANES, LANES), lambda i, te, nv: (i, 0)),
    )
    return pl.pallas_call(
        _expert_kernel,
        out_shape=jax.ShapeDtypeStruct((rows_pad * SUBLANES, LANES), F32),
        grid_spec=grid_spec,
        compiler_params=_params("arbitrary"),
        name="experts",
    )(tile_expert, n_valid, hs, wg, wu, wd)


def _combine_kernel(pos_ref, y_hbm, x_ref, w_ref, gf_ref, lng_ref, lnb_ref, o_ref, buf, sem, *, n_tok):
    i = pl.program_id(0)
    n = pl.num_programs(0)
    slot = i % 2
    tm = x_ref.shape[0]

    def issue(tile, slot_):
        base = tile * tm

        def body(r, carry):
            for k in range(TOP_K):
                pltpu.make_async_copy(_row_tile(y_hbm, pos_ref[k * n_tok + base + r]),
                                      _row_tile(buf.at[slot_, k], r * SUBLANES),
                                      sem.at[slot_]).start(priority=k % N_DMA_PRIORITIES)
            return carry

        lax.fori_loop(0, tm, body, 0, unroll=8)

    @pl.when(i == 0)
    def _():
        issue(0, 0)

    @pl.when(i + 1 < n)
    def _():
        issue(i + 1, 1 - slot)

    for k in range(TOP_K):
        pltpu.make_async_copy(y_hbm.at[pl.ds(0, tm * SUBLANES)], buf.at[slot, k], sem.at[slot]).wait()
    w = w_ref[...]
    m = (w[:, 0:1] * _load_token_tiled(buf.at[slot, 0], tm)
         + w[:, 1:2] * _load_token_tiled(buf.at[slot, 1], tm))
    z = DEEPNORM_ALPHA * x_ref[...] + (1.0 + gf_ref[0]) * m
    o_ref[...] = _layer_norm(z, lng_ref[...], lnb_ref[...])


def _combine(pos, y, x, wts_t, g_f, ln_g, ln_b, seq):
    t, d = x.shape
    tm = TM_COMBINE
    tiles_per_seq = seq // tm
    grid_spec = pltpu.PrefetchScalarGridSpec(
        num_scalar_prefetch=1,
        grid=(t // tm,),
        in_specs=[
            pl.BlockSpec(memory_space=pl.ANY),
            pl.BlockSpec((tm, d), lambda i, pos_: (i, 0)),
            pl.BlockSpec((tm, TOP_K), lambda i, pos_: (i, 0)),
            pl.BlockSpec((1, 1, d), lambda i, pos_: (i // tiles_per_seq, 0, 0)),
            pl.BlockSpec((1, d), lambda i, pos_: (0, 0)),
            pl.BlockSpec((1, d), lambda i, pos_: (0, 0)),
        ],
        out_specs=pl.BlockSpec((tm, d), lambda i, pos_: (i, 0)),
        scratch_shapes=[pltpu.VMEM((2, TOP_K, tm * SUBLANES, LANES), F32),
                        pltpu.SemaphoreType.DMA((2,))],
    )
    return pl.pallas_call(
        functools.partial(_combine_kernel, n_tok=t),
        out_shape=jax.ShapeDtypeStruct((t, d), F32),
        grid_spec=grid_spec,
        compiler_params=_params("arbitrary"),
        name="combine",
    )(pos, y, x, wts_t, g_f, ln_g, ln_b)


def kernel(x, c, ada_w, ada_b, ln_attn_g, ln_attn_b, ln_ffn_g, ln_ffn_b, fox_w_in, fox_b_f, fox_w_o,
           kv_ada_w, kv_ada_b, kv_w, diff_w_q, diff_lambda, diff_norm_g, diff_w_o, router_w, router_b,
           moe_w_gate, moe_w_up, moe_w_down):
    bsz, seq, d = x.shape
    t = bsz * seq
    rows_pad = TOP_K * t + N_EXPERTS * TM_EXPERT
    nq = seq // TQ

    mod = _ada(c, ada_w, ada_b, 1536)
    kv_mod = _ada(c, kv_ada_w[None], kv_ada_b[None], 1024)[0]

    def mod_part(l, k):
        return mod[l, :, k * d:(k + 1) * d].reshape(bsz, 1, d)

    router_wt = router_w.T
    router_bc = router_b.reshape(N_EXPERTS, 1)
    xt = x.reshape(t, d)

    k_sh = v_sh_t = None
    alibi_kb = _alibi_key_bias(seq)
    for l in range(DEPTH):
        sh_a, sc_a, g_a, sh_f, sc_f, g_f = (mod_part(l, k) for k in range(6))
        if l < N_A:
            w_in = fox_w_in[l]
            w_qt = w_in[:, :d].T.astype(BF16)
            w_k = w_in[:, d:2 * d].astype(BF16)
            w_vt = w_in[:, 2 * d:3 * d].T.astype(BF16)
            w_f = jnp.pad(w_in[:, 3 * d:], ((0, 0), (0, HEAD_BLOCK - FOX_HEADS))).astype(BF16)
            b_f = jnp.pad(fox_b_f[l], (0, HEAD_BLOCK - FOX_HEADS)).reshape(1, HEAD_BLOCK)
            k_arr, q_t, v_t, key_bias, cum_t = _modproj(xt, sh_a, sc_a, [w_k], [w_qt, w_vt], seq,
                                                        forget=(w_f, b_f))
            o = _attention(
                "fox", q_t, k_arr, 0, key_bias,
                pl.BlockSpec((seq, HEAD_BLOCK), lambda b, h, i: (b, h)),
                v_t, 0, (cum_t.reshape(N_HEAD_BLOCKS, 2, t),),
                [pl.BlockSpec((1, 2, TQ), lambda b, h, i: (h, 0, b * nq + i))],
                bsz, seq)
            w_o = fox_w_o[l].astype(BF16)
        else:
            j = l - N_A
            if l == N_A:
                k_sh, v_sh_t = _modproj(xt, kv_mod[:, :d].reshape(bsz, 1, d),
                                        kv_mod[:, d:].reshape(bsz, 1, d),
                                        [kv_w[:, :d].astype(BF16)], [kv_w[:, d:].T.astype(BF16)], seq)
            (q_t,) = _modproj(xt, sh_a, sc_a, [], [diff_w_q[j].T.astype(BF16)], seq)
            lam_init = 0.8 - 0.6 * math.exp(-0.3 * l)
            o = _attention(
                "diff", q_t, k_sh, 0, alibi_kb,
                pl.BlockSpec((seq, HEAD_BLOCK), lambda b, h, i: (0, 0)),
                v_sh_t, 0,
                (diff_lambda[j][None], diff_norm_g[j].reshape(1, HEAD_BLOCK)),
                [pl.BlockSpec((1, 4, HALF), lambda b, h, i: (0, 0, 0)),
                 pl.BlockSpec((1, HEAD_BLOCK), lambda b, h, i: (0, 0))],
                bsz, seq, lam_init=lam_init)
            w_o = diff_w_o[j].astype(BF16)

        xt, h, eidx, wts, rank, counts = _post_attn(
            o, xt, w_o, g_a, ln_attn_g[l].reshape(1, d), ln_attn_b[l].reshape(1, d), sh_f, sc_f,
            router_wt, router_bc, seq)
        pos, pad_lo, pad_hi, tile_expert, n_valid = _routing_tables(
            eidx, rank, counts[:, 0].astype(jnp.int32), rows_pad)
        pos8 = pos * SUBLANES
        hs = _dispatch(h, pos8, pad_lo, pad_hi, rows_pad)
        y = _experts(hs, tile_expert, n_valid, moe_w_gate[l].astype(BF16),
                     moe_w_up[l].astype(BF16), moe_w_down[l].astype(BF16))
        xt = _combine(pos8, y, xt, wts.T, g_f, ln_ffn_g[l].reshape(1, d),
                      ln_ffn_b[l].reshape(1, d), seq)
    return xt.reshape(bsz, seq, d)
```

```python
import functools
import math

import jax
import jax.numpy as jnp
import numpy as np
from jax import lax
from jax.experimental import pallas as pl
from jax.experimental.pallas import tpu as pltpu

F32 = jnp.float32
BF16 = jnp.bfloat16
HIGHEST = lax.Precision.HIGHEST

D_MODEL = 1024
DEPTH = 4
N_A = DEPTH // 2
HEAD_BLOCK = 128
N_HEAD_BLOCKS = D_MODEL // HEAD_BLOCK
HALF = HEAD_BLOCK // 2
FOX_HEADS = 16
DIFF_HEADS = 8
N_EXPERTS = 16
N_GROUPS = 4
EXPERTS_PER_GROUP = N_EXPERTS // N_GROUPS
TOP_K = 2
D_EXPERT = 512
DEEPNORM_ALPHA = (2 * DEPTH) ** 0.25
LN_EPS = 1e-5
NEG_INF = -1e30
LOG2E = math.log2(math.e)

VMEM_LIMIT = 56 * 1024 * 1024

TM_PROJ = 512
TN_PROJ = 256
TQ = 512
TK = 512
DIAG_STRIPS = 2
BLOCKS_PER_TRIP = 4
LOOKAHEAD = 3
BIAS_ROWS = 16
ONES_ROWS = 16
TM_EXPERT = 512
TM_COMBINE = 256
TM_DISPATCH = 1024

NT_DIMS = (((1,), (1,)), ((), ()))


def _params(*sem, flags=None):
    return pltpu.CompilerParams(dimension_semantics=sem, vmem_limit_bytes=VMEM_LIMIT, flags=flags)


def _split3(v):
    hi = v.astype(BF16).astype(F32)
    r = v - hi
    mid = r.astype(BF16).astype(F32)
    lo = (r - mid).astype(BF16).astype(F32)
    return hi, mid, lo


def _ada_kernel(c_ref, w_ref, b_ref, o_ref):
    c = c_ref[...]
    sc = c * (1.0 / (1.0 + jnp.exp(-c)))
    o_ref[0] = jnp.dot(sc, w_ref[0], precision=HIGHEST, preferred_element_type=F32) + b_ref[0]


def _ada(c, w, b, tn):
    n_layers, d, n = w.shape
    bsz = c.shape[0]
    return pl.pallas_call(
        _ada_kernel,
        out_shape=jax.ShapeDtypeStruct((n_layers, bsz, n), F32),
        grid=(n_layers, n // tn),
        in_specs=[
            pl.BlockSpec((bsz, d), lambda l, j: (0, 0)),
            pl.BlockSpec((1, d, tn), lambda l, j: (l, 0, j)),
            pl.BlockSpec((1, 1, tn), lambda l, j: (l, 0, j)),
        ],
        out_specs=pl.BlockSpec((1, bsz, tn), lambda l, j: (l, 0, j)),
        compiler_params=_params("parallel", "parallel"),
        name="ada_params",
    )(c, w, b.reshape(n_layers, 1, n))


def _modproj_kernel(*refs, n_std, n_t, with_forget, tiles_per_seq):
    x_ref, sh_ref, sc_ref = refs[:3]
    pos = 3
    wstd = refs[pos:pos + n_std]; pos += n_std
    wt = refs[pos:pos + n_t]; pos += n_t
    if with_forget:
        wf_ref, bf_ref, sel_ref, ones_ref = refs[pos:pos + 4]; pos += 4
    ostd = refs[pos:pos + n_std]; pos += n_std
    ot = refs[pos:pos + n_t]; pos += n_t
    if with_forget:
        kb_ref, cumt_ref, carry_ref = refs[pos:pos + 3]

    hb = (x_ref[...] * (1.0 + sc_ref[0]) + sh_ref[0]).astype(BF16)
    for w_ref, o_ref in zip(wstd, ostd):
        for j in range(w_ref.shape[1] // TN_PROJ):
            cols = slice(j * TN_PROJ, (j + 1) * TN_PROJ)
            o_ref[:, cols] = jnp.dot(hb, w_ref[:, cols], preferred_element_type=F32).astype(o_ref.dtype)
    for w_ref, o_ref in zip(wt, ot):
        for j in range(w_ref.shape[0] // TN_PROJ):
            rows = slice(j * TN_PROJ, (j + 1) * TN_PROJ)
            o_ref[rows, :] = lax.dot_general(w_ref[rows, :], hb, NT_DIMS,
                                             preferred_element_type=F32).astype(o_ref.dtype)

    if with_forget:
        @pl.when(pl.program_id(0) % tiles_per_seq == 0)
        def _():
            carry_ref[...] = jnp.zeros_like(carry_ref)

        z = jnp.dot(hb, wf_ref[...], preferred_element_type=F32) + bf_ref[...]
        logf = jnp.minimum(z, 0.0) - jnp.log(1.0 + jnp.exp(-jnp.abs(z)))
        tm = z.shape[0]
        lower = (lax.broadcasted_iota(jnp.int32, (tm, tm), 0)
                 >= lax.broadcasted_iota(jnp.int32, (tm, tm), 1)).astype(F32)
        cum = jnp.dot(lower, logf, precision=HIGHEST, preferred_element_type=F32) + carry_ref[...]
        cumt_ref[...] = cum.T[:FOX_HEADS, :]
        carry_ref[...] = cum[tm - 1:tm, :]
        terms = jnp.concatenate([v.astype(BF16) for v in _split3((-LOG2E) * cum)], axis=1)
        for j in range(sel_ref.shape[1] // TN_PROJ):
            cols = slice(j * TN_PROJ, (j + 1) * TN_PROJ)
            kb_ref[:, cols] = (jnp.dot(terms, sel_ref[:, cols], preferred_element_type=F32)
                               + ones_ref[:, cols]).astype(kb_ref.dtype)


def _fox_bias_selection():
    sel = np.zeros((3 * HEAD_BLOCK, D_MODEL), np.float32)
    ones = np.zeros((1, D_MODEL), np.float32)
    for hp in range(N_HEAD_BLOCKS):
        for a in range(2):
            for term in range(3):
                sel[term * HEAD_BLOCK + 2 * hp + a, hp * HEAD_BLOCK + 6 * a + term] = 1.0
            ones[0, hp * HEAD_BLOCK + 6 * a + 3:hp * HEAD_BLOCK + 6 * a + 6] = 1.0
    return jnp.asarray(sel, BF16), jnp.asarray(ones, F32)


def _modproj(x, shift, scale, w_std, w_t, seq, forget=None):
    t, d = x.shape
    tm = TM_PROJ
    tiles_per_seq = seq // tm
    const = lambda i: (0, 0)
    mod_spec = pl.BlockSpec((1, 1, d), lambda i: (i // tiles_per_seq, 0, 0))
    in_specs = [pl.BlockSpec((tm, d), lambda i: (i, 0)), mod_spec, mod_spec]
    in_specs += [pl.BlockSpec(w.shape, const) for w in w_std]
    in_specs += [pl.BlockSpec(w.shape, const) for w in w_t]
    out_shape = [jax.ShapeDtypeStruct((t, w.shape[1]), BF16) for w in w_std]
    out_shape += [jax.ShapeDtypeStruct((w.shape[0], t), BF16) for w in w_t]
    out_specs = [pl.BlockSpec((tm, w.shape[1]), lambda i: (i, 0)) for w in w_std]
    out_specs += [pl.BlockSpec((w.shape[0], tm), lambda i: (0, i)) for w in w_t]
    args = [x, shift, scale, *w_std, *w_t]
    scratch = []
    if forget is not None:
        wf, bf = forget
        sel, ones = _fox_bias_selection()
        in_specs += [pl.BlockSpec(a.shape, const) for a in (wf, bf, sel, ones)]
        args += [wf, bf, sel, ones]
        out_shape += [jax.ShapeDtypeStruct((t, d), BF16), jax.ShapeDtypeStruct((FOX_HEADS, t), F32)]
        out_specs += [pl.BlockSpec((tm, d), lambda i: (i, 0)),
                      pl.BlockSpec((FOX_HEADS, tm), lambda i: (0, i))]
        scratch = [pltpu.VMEM((1, HEAD_BLOCK), F32)]
    return pl.pallas_call(
        functools.partial(_modproj_kernel, n_std=len(w_std), n_t=len(w_t),
                          with_forget=forget is not None, tiles_per_seq=tiles_per_seq),
        out_shape=tuple(out_shape),
        grid=(t // tm,),
        in_specs=in_specs,
        out_specs=tuple(out_specs),
        scratch_shapes=scratch,
        compiler_params=_params("arbitrary"),
        name="modproj_fox" if forget is not None else "modproj",
    )(*args)


def _bias_rows(entries, tq):
    rid = lax.broadcasted_iota(jnp.int32, (BIAS_ROWS, tq), 0)
    out = jnp.zeros((BIAS_ROWS, tq), F32)
    for r, v in entries.items():
        out = jnp.where(rid == r, v, out)
    return out


def _attn_kernel(*refs, mode, lam_init, n_heads):
    refs = list(refs)
    q_at, o_at = 0, (5 if mode == "fox" else 6)
    q_ref, o_ref = refs[q_at], refs[o_at]

    def query_tile(i, carry):
        rows = pl.ds(pl.multiple_of(i * TQ, TQ), TQ)
        tile_refs = list(refs)
        tile_refs[q_at] = q_ref.at[:, rows]
        tile_refs[o_at] = o_ref.at[rows, :]
        if mode == "fox":
            tile_refs[4] = refs[4].at[:, :, rows]
        _attn_query_tile(i, *tile_refs, mode=mode, lam_init=lam_init, n_heads=n_heads)
        return carry

    lax.fori_loop(0, q_ref.shape[1] // TQ, query_tile, 0)


def _attn_query_tile(i, *refs, mode, lam_init, n_heads):
    if mode == "fox":
        q_ref, k_ref, kb_ref, v_ref, cumt_ref, o_ref, m_ref, acc_ref = refs
    else:
        q_ref, k_ref, kb_ref, v_ref, lam_ref, g_ref, o_ref, m_ref, acc_ref = refs
    hp = pl.program_id(1)
    tq = q_ref.shape[1]

    qs = q_ref[...] * jnp.asarray(HALF ** -0.5 * LOG2E, BF16)
    zeros_half = jnp.zeros((HALF, tq), BF16)
    q_tops = (jnp.concatenate([qs[:HALF], zeros_half], axis=0),
              jnp.concatenate([zeros_half, qs[HALF:]], axis=0))
    pad = jnp.zeros((HEAD_BLOCK - BIAS_ROWS, tq), BF16)
    if mode == "fox":
        q_bias = []
        for a in range(2):
            c = _split3(cumt_ref[0, a:a + 1, :] * LOG2E)
            base = 6 * a
            rows = {base: 1.0, base + 1: 1.0, base + 2: 1.0,
                    base + 3: c[0], base + 4: c[1], base + 5: c[2]}
            q_bias.append(_bias_rows(rows, tq).astype(BF16))
    else:
        slope = jnp.exp2(jnp.full((1, 1), -8.0, F32) * (hp + 1).astype(F32) / n_heads)
        l3 = _split3(slope * LOG2E)
        qpos = (i * tq + lax.broadcasted_iota(jnp.int32, (1, tq), 1)).astype(F32)
        t3 = _split3(-(slope * LOG2E) * qpos)
        rows = {0: l3[0], 1: l3[0], 2: l3[1], 3: l3[1], 4: l3[2], 5: l3[2],
                6: t3[0], 7: t3[1], 8: t3[2]}
        q_bias = [_bias_rows(rows, tq).astype(BF16)] * 2
    q_aug = tuple(jnp.concatenate([q_tops[a], q_bias[a], pad], axis=0) for a in range(2))

    m_ref[...] = jnp.full_like(m_ref, NEG_INF)
    acc_ref[...] = jnp.zeros_like(acc_ref)
    ones_rows = jnp.ones((ONES_ROWS, TK), BF16)

    def key_slice(k0, nk):
        return pl.ds(pl.multiple_of(k0, nk), nk)

    def scores(a, k0, nk, q0, nq):
        ks = key_slice(k0, nk)
        k_aug = jnp.concatenate([k_ref[ks, :], kb_ref[ks, :]], axis=1)
        return jnp.dot(k_aug, q_aug[a][:, q0:q0 + nq], preferred_element_type=F32)

    v_rows = [slice(a * HALF, (a + 1) * HALF) if mode == "fox" else slice(0, HEAD_BLOCK)
              for a in range(2)]
    n_v = v_rows[0].stop - v_rows[0].start

    def accumulate(a, k0, nk, q0, nq, masked, s):
        ql = slice(q0, q0 + nq)
        v_aug = jnp.concatenate([v_ref[v_rows[a], key_slice(k0, nk)], ones_rows[:, :nk]], axis=0)
        if masked:
            keep = ((k0 + lax.broadcasted_iota(jnp.int32, s.shape, 0))
                    <= (i * tq + q0 + lax.broadcasted_iota(jnp.int32, s.shape, 1)))
            s = jnp.where(keep, s, NEG_INF)
        m_old = m_ref[a, :, ql]
        m_new = jnp.maximum(m_old, jnp.max(s, axis=0, keepdims=True))
        alpha = jnp.exp2(m_old - m_new)
        p = jnp.exp2(s - m_new).astype(BF16)
        acc_ref[a, :, ql] = alpha * acc_ref[a, :, ql] + jnp.dot(v_aug, p, preferred_element_type=F32)
        m_ref[a, :, ql] = m_new

    def full_block(j):
        return [(a, j * TK, TK, 0, tq, False) for a in range(2)]

    def diagonal_block(j):
        strip = TK // DIAG_STRIPS
        return [(a, j * TK + c * strip, strip, c * strip, tq - c * strip, True)
                for c in range(DIAG_STRIPS) for a in range(2)]

    def process(work):
        pending = [scores(*w[:5]) for w in work[:LOOKAHEAD]]
        for n, w in enumerate(work):
            s = pending.pop(0)
            if n + LOOKAHEAD < len(work):
                pending.append(scores(*work[n + LOOKAHEAD][:5]))
            accumulate(*w, s)

    n_full = (i * tq) // TK

    def loop_body(p, carry):
        process(sum((full_block(BLOCKS_PER_TRIP * p + d) for d in range(BLOCKS_PER_TRIP)), []))
        return carry

    lax.fori_loop(0, n_full // BLOCKS_PER_TRIP, loop_body, 0)

    for rem in range(BLOCKS_PER_TRIP):
        @pl.when(n_full % BLOCKS_PER_TRIP == rem)
        def _(rem=rem):
            process(sum((full_block(n_full - rem + d) for d in range(rem)), [])
                    + diagonal_block(n_full))

    inv = [1.0 / acc_ref[a, n_v:n_v + 1, :] for a in range(2)]
    if mode == "fox":
        o_t = jnp.concatenate([acc_ref[a, :n_v, :] * inv[a] for a in range(2)], axis=0)
        o_ref[...] = o_t.T.astype(o_ref.dtype)
    else:
        lp = lam_ref[0]
        lam = (jnp.exp(jnp.sum(lp[0:1] * lp[1:2], axis=-1, keepdims=True))
               - jnp.exp(jnp.sum(lp[2:3] * lp[3:4], axis=-1, keepdims=True)) + lam_init)
        o_t = acc_ref[0, :HEAD_BLOCK, :] * inv[0] - lam * (acc_ref[1, :HEAD_BLOCK, :] * inv[1])
        o = o_t.T
        o = o * lax.rsqrt(jnp.mean(o * o, axis=-1, keepdims=True) + LN_EPS)
        o_ref[...] = (o * g_ref[...] * (1.0 - lam_init)).astype(o_ref.dtype)


def _attention(mode, q_t, k_arr, k_off, kb, kb_spec, v_t, v_off, extras, extra_specs, batch, seq,
               lam_init=0.0, n_heads=DIFF_HEADS):
    assert TQ == TK, "one masked diagonal block per query tile"
    t = batch * seq
    acc_rows = (HALF if mode == "fox" else HEAD_BLOCK) + ONES_ROWS
    in_specs = [
        pl.BlockSpec((HEAD_BLOCK, seq), lambda b, h: (h, b)),
        pl.BlockSpec((seq, HEAD_BLOCK), lambda b, h: (b, k_off + h)),
        kb_spec,
        pl.BlockSpec((HEAD_BLOCK, seq), lambda b, h: (v_off + h, b)),
    ] + extra_specs
    return pl.pallas_call(
        functools.partial(_attn_kernel, mode=mode, lam_init=lam_init, n_heads=n_heads),
        out_shape=jax.ShapeDtypeStruct((t, D_MODEL), BF16),
        grid=(batch, N_HEAD_BLOCKS),
        in_specs=in_specs,
        out_specs=pl.BlockSpec((seq, HEAD_BLOCK), lambda b, h: (b, h)),
        scratch_shapes=[pltpu.VMEM((2, 1, TQ), F32),
                        pltpu.VMEM((2, acc_rows, TQ), F32)],
        compiler_params=_params("parallel", "parallel"),
        name="attn_" + mode,
    )(q_t, k_arr, kb, v_t, *extras)


def _bias_lanes(cols):
    lanes = jnp.stack(cols, axis=-1)
    pad = [(0, 0)] * (lanes.ndim - 1) + [(0, HEAD_BLOCK - lanes.shape[-1])]
    return jnp.pad(lanes, pad).astype(BF16)


def _alibi_key_bias(seq):
    kpos = jnp.arange(seq, dtype=jnp.int32)
    hi = ((kpos // 64) * 64).astype(F32)
    lo = (kpos % 64).astype(F32)
    one = jnp.ones((seq,), F32)
    return _bias_lanes([hi, lo, hi, lo, hi, lo, one, one, one])


SUBLANES = 8
LANES = 128
N_DMA_PRIORITIES = 2


def _store_token_tiled(ref, value):
    n = value.shape[0]
    for s in range(SUBLANES):
        ref[pl.ds(s, n, stride=SUBLANES), :] = value[:, s * LANES:(s + 1) * LANES]


def _load_token_tiled(ref, n):
    return jnp.concatenate([ref[pl.ds(s, n, stride=SUBLANES), :] for s in range(SUBLANES)], axis=1)


def _layer_norm(z, g, b):
    mu = jnp.mean(z, axis=-1, keepdims=True)
    zc = z - mu
    var = jnp.mean(zc * zc, axis=-1, keepdims=True)
    return zc * lax.rsqrt(var + LN_EPS) * g + b


def _argmax_first(vals):
    best, idx = vals[0], jnp.zeros(vals[0].shape, jnp.int32)
    for j in range(1, len(vals)):
        better = vals[j] > best
        idx = jnp.where(better, j, idx)
        best = jnp.where(better, vals[j], best)
    return idx


def _select(vals, idx):
    out = vals[0]
    for j in range(1, len(vals)):
        out = jnp.where(idx == j, vals[j], out)
    return out


def _post_attn_kernel(o_ref, x_ref, wo_ref, ga_ref, lng_ref, lnb_ref, shf_ref, scf_ref,
                      rwt_ref, rb_ref, upper_ref, xo_ref, h_ref, eidx_ref, wts_ref, rank_ref,
                      counts_ref, count_ref):
    y = jnp.dot(o_ref[...], wo_ref[...], preferred_element_type=F32)
    z = DEEPNORM_ALPHA * x_ref[...] + (1.0 + ga_ref[0]) * y
    xn = _layer_norm(z, lng_ref[...], lnb_ref[...])
    xo_ref[...] = xn
    h = xn * (1.0 + scf_ref[0]) + shf_ref[0]
    _store_token_tiled(h_ref, h)

    logits = lax.dot_general(rwt_ref[...], h, NT_DIMS, precision=HIGHEST,
                             preferred_element_type=F32)
    rows = [logits[e:e + 1, :] for e in range(N_EXPERTS)]
    mx = functools.reduce(jnp.maximum, rows)
    ex = [jnp.exp(r - mx) for r in rows]
    den = functools.reduce(lambda a, b: a + b, ex)
    probs = [e_ / den for e_ in ex]
    sel = [probs[e] + rb_ref[e:e + 1, :] for e in range(N_EXPERTS)]

    grp_scores = []
    for g in range(N_GROUPS):
        v = sel[g * EXPERTS_PER_GROUP:(g + 1) * EXPERTS_PER_GROUP]
        pair_sums = [v[a] + v[b] for a in range(EXPERTS_PER_GROUP) for b in range(a + 1, EXPERTS_PER_GROUP)]
        grp_scores.append(functools.reduce(jnp.maximum, pair_sums))
    g_idx = _argmax_first(grp_scores)

    in_sel = [_select([sel[g * EXPERTS_PER_GROUP + j] for g in range(N_GROUPS)], g_idx)
              for j in range(EXPERTS_PER_GROUP)]
    in_prob = [_select([probs[g * EXPERTS_PER_GROUP + j] for g in range(N_GROUPS)], g_idx)
               for j in range(EXPERTS_PER_GROUP)]
    i1 = _argmax_first(in_sel)
    b2 = jnp.full(in_sel[0].shape, -jnp.inf, F32)
    i2 = jnp.zeros(in_sel[0].shape, jnp.int32)
    for j in range(EXPERTS_PER_GROUP):
        cand = (i1 != j) & (in_sel[j] > b2)
        i2 = jnp.where(cand, j, i2)
        b2 = jnp.where(cand, in_sel[j], b2)
    p1 = _select(in_prob, i1)
    p2 = _select(in_prob, i2)
    tot = p1 + p2
    experts = (g_idx * EXPERTS_PER_GROUP + i1, g_idx * EXPERTS_PER_GROUP + i2)
    wts_ref[0:1, :] = p1 / tot
    wts_ref[1:2, :] = p2 / tot

    @pl.when(pl.program_id(0) == 0)
    def _():
        count_ref[...] = jnp.zeros_like(count_ref)

    tm = experts[0].shape[1]
    eid = lax.broadcasted_iota(jnp.int32, (N_EXPERTS, tm), 0)
    hit = [eid == e for e in experts]
    chosen = jnp.where(hit[0] | hit[1], 1.0, 0.0)
    before = jnp.dot(chosen.astype(BF16), upper_ref[...], preferred_element_type=F32) + count_ref[...]
    for k in range(TOP_K):
        eidx_ref[k:k + 1, :] = experts[k]
        rank_ref[k:k + 1, :] = jnp.sum(jnp.where(hit[k], before, 0.0), axis=0,
                                       keepdims=True).astype(jnp.int32)
    count_ref[...] += jnp.sum(chosen, axis=1, keepdims=True)
    counts_ref[...] = jnp.broadcast_to(count_ref[...], counts_ref.shape)


def _post_attn(o, x, wo, g_a, ln_g, ln_b, sh_f, sc_f, router_wt, router_b, seq):
    t, d = x.shape
    tm = TM_PROJ
    tiles_per_seq = seq // tm
    mod_spec = pl.BlockSpec((1, 1, d), lambda i: (i // tiles_per_seq, 0, 0))
    vec_spec = pl.BlockSpec((1, d), lambda i: (0, 0))
    tok_spec = pl.BlockSpec((tm, d), lambda i: (i, 0))
    route_spec = pl.BlockSpec((TOP_K, tm), lambda i: (0, i))
    upper = jnp.asarray(np.triu(np.ones((tm, tm), np.float32), k=1), BF16)
    return pl.pallas_call(
        _post_attn_kernel,
        out_shape=(jax.ShapeDtypeStruct((t, d), F32), jax.ShapeDtypeStruct((t * SUBLANES, LANES), F32),
                   jax.ShapeDtypeStruct((TOP_K, t), jnp.int32), jax.ShapeDtypeStruct((TOP_K, t), F32),
                   jax.ShapeDtypeStruct((TOP_K, t), jnp.int32),
                   jax.ShapeDtypeStruct((N_EXPERTS, LANES), F32)),
        grid=(t // tm,),
        in_specs=[
            tok_spec, tok_spec,
            pl.BlockSpec((d, d), lambda i: (0, 0)),
            mod_spec, vec_spec, vec_spec, mod_spec, mod_spec,
            pl.BlockSpec((N_EXPERTS, d), lambda i: (0, 0)),
            pl.BlockSpec((N_EXPERTS, 1), lambda i: (0, 0)),
            pl.BlockSpec((tm, tm), lambda i: (0, 0)),
        ],
        out_specs=(tok_spec, pl.BlockSpec((tm * SUBLANES, LANES), lambda i: (i, 0)),
                   route_spec, route_spec, route_spec,
                   pl.BlockSpec((N_EXPERTS, LANES), lambda i: (0, 0))),
        scratch_shapes=[pltpu.VMEM((N_EXPERTS, 1), F32)],
        compiler_params=_params("arbitrary"),
        name="post_attn",
    )(o, x, wo, g_a, ln_g, ln_b, sh_f, sc_f, router_wt, router_b, upper)


def _routing_tables(eidx, rank, counts, rows_pad):
    t = eidx.shape[1]
    padded = ((counts + TM_EXPERT - 1) // TM_EXPERT) * TM_EXPERT
    ends = jnp.cumsum(padded)
    starts = ends - padded
    start_of = sum(jnp.where(eidx == e, starts[e], 0) for e in range(N_EXPERTS))
    pos = (start_of + rank).reshape(-1)
    n_tiles = rows_pad // TM_EXPERT
    tile_start = jnp.arange(n_tiles, dtype=jnp.int32) * TM_EXPERT
    tile_expert = jnp.minimum(jnp.sum((tile_start[:, None] >= ends[None, :]).astype(jnp.int32), axis=1),
                              N_EXPERTS - 1).astype(jnp.int32)
    n_valid = (ends[-1] // TM_EXPERT).astype(jnp.int32).reshape(1)
    pad_lo = (starts + counts).astype(jnp.int32)
    pad_hi = jnp.concatenate([ends[:-1], jnp.full((1,), rows_pad, ends.dtype)]).astype(jnp.int32)
    return pos.astype(jnp.int32), pad_lo, pad_hi, tile_expert, n_valid


def _row_tile(ref, row8):
    return ref.at[pl.ds(pl.multiple_of(row8, SUBLANES), SUBLANES)]


def _dispatch_kernel(pos8_ref, pad_lo_ref, pad_hi_ref, h_ref, hs_hbm, zero_ref, sem, pad_sem, *, n_tok):
    i = pl.program_id(0)
    tm = h_ref.shape[0] // SUBLANES
    base = i * tm

    def body(r, carry):
        for k in range(TOP_K):
            pltpu.make_async_copy(_row_tile(h_ref, r * SUBLANES),
                                  _row_tile(hs_hbm, pos8_ref[k * n_tok + base + r]),
                                  sem).start(priority=k % N_DMA_PRIORITIES)
        return carry

    lax.fori_loop(0, tm, body, 0, unroll=8)

    @pl.when(i == 0)
    def _():
        zero_ref[...] = jnp.zeros_like(zero_ref)
        for e in range(N_EXPERTS):
            lo, hi = pad_lo_ref[e], pad_hi_ref[e]

            def start(r, carry):
                pltpu.make_async_copy(zero_ref, _row_tile(hs_hbm, r * SUBLANES), pad_sem).start()
                return carry

            def wait(r, carry):
                pltpu.make_async_copy(zero_ref, _row_tile(hs_hbm, r * SUBLANES), pad_sem).wait()
                return carry

            lax.fori_loop(lo, hi, start, 0)
            lax.fori_loop(lo, hi, wait, 0)

    for k in range(TOP_K):
        pltpu.make_async_copy(h_ref, hs_hbm.at[pl.ds(0, tm * SUBLANES)], sem).wait()


def _dispatch(h_tiled, pos8, pad_lo, pad_hi, rows_pad):
    t = h_tiled.shape[0] // SUBLANES
    tm = TM_DISPATCH
    grid_spec = pltpu.PrefetchScalarGridSpec(
        num_scalar_prefetch=3,
        grid=(t // tm,),
        in_specs=[pl.BlockSpec((tm * SUBLANES, LANES), lambda i, *_: (i, 0))],
        out_specs=pl.BlockSpec(memory_space=pl.ANY),
        scratch_shapes=[pltpu.VMEM((SUBLANES, LANES), F32), pltpu.SemaphoreType.DMA(()),
                        pltpu.SemaphoreType.DMA(())],
    )
    return pl.pallas_call(
        functools.partial(_dispatch_kernel, n_tok=t),
        out_shape=jax.ShapeDtypeStruct((rows_pad * SUBLANES, LANES), F32),
        grid_spec=grid_spec,
        compiler_params=_params("arbitrary"),
        name="dispatch",
    )(pos8, pad_lo, pad_hi, h_tiled)


def _expert_kernel(te_ref, nv_ref, hs_ref, wg_ref, wu_ref, wd_ref, y_ref):
    i = pl.program_id(0)
    nv = nv_ref[0]

    @pl.when(i < nv)
    def _():
        hb = _load_token_tiled(hs_ref, TM_EXPERT).astype(BF16)
        g = jnp.dot(hb, wg_ref[0], preferred_element_type=F32)
        u = jnp.dot(hb, wu_ref[0], preferred_element_type=F32)
        a = g * (1.0 / (1.0 + jnp.exp(-g))) * u
        _store_token_tiled(y_ref, jnp.dot(a.astype(BF16), wd_ref[0], preferred_element_type=F32))

    @pl.when(i >= nv)
    def _():
        y_ref[...] = jnp.zeros_like(y_ref)


def _experts(hs, tile_expert, n_valid, wg, wu, wd):
    d = wg.shape[1]
    rows_pad = hs.shape[0] // SUBLANES
    n_tiles = rows_pad // TM_EXPERT
    grid_spec = pltpu.PrefetchScalarGridSpec(
        num_scalar_prefetch=2,
        grid=(n_tiles,),
        in_specs=[
            pl.BlockSpec((TM_EXPERT * SUBLANES, LANES), lambda i, te, nv: (i, 0)),
            pl.BlockSpec((1, d, D_EXPERT), lambda i, te, nv: (te[i], 0, 0)),
            pl.BlockSpec((1, d, D_EXPERT), lambda i, te, nv: (te[i], 0, 0)),
            pl.BlockSpec((1, D_EXPERT, d), lambda i, te, nv: (te[i], 0, 0)),
        ],
        out_specs=pl.BlockSpec((TM_EXPERT * SUBLANES, LANES), lambda i, te, nv: (i, 0)),
    )
    return pl.pallas_call(
        _expert_kernel,
        out_shape=jax.ShapeDtypeStruct((rows_pad * SUBLANES, LANES), F32),
        grid_spec=grid_spec,
        compiler_params=_params("arbitrary"),
        name="experts",
    )(tile_expert, n_valid, hs, wg, wu, wd)


def _combine_kernel(pos_ref, y_hbm, x_ref, w_ref, gf_ref, lng_ref, lnb_ref, o_ref, buf, sem, *, n_tok):
    i = pl.program_id(0)
    n = pl.num_programs(0)
    slot = i % 2
    tm = x_ref.shape[0]

    def issue(tile, slot_):
        base = tile * tm

        def body(r, carry):
            for k in range(TOP_K):
                pltpu.make_async_copy(_row_tile(y_hbm, pos_ref[k * n_tok + base + r]),
                                      _row_tile(buf.at[slot_, k], r * SUBLANES),
                                      sem.at[slot_]).start(priority=k % N_DMA_PRIORITIES)
            return carry

        lax.fori_loop(0, tm, body, 0, unroll=8)

    @pl.when(i == 0)
    def _():
        issue(0, 0)

    @pl.when(i + 1 < n)
    def _():
        issue(i + 1, 1 - slot)

    for k in range(TOP_K):
        pltpu.make_async_copy(y_hbm.at[pl.ds(0, tm * SUBLANES)], buf.at[slot, k], sem.at[slot]).wait()
    w = w_ref[...]
    m = (w[:, 0:1] * _load_token_tiled(buf.at[slot, 0], tm)
         + w[:, 1:2] * _load_token_tiled(buf.at[slot, 1], tm))
    z = DEEPNORM_ALPHA * x_ref[...] + (1.0 + gf_ref[0]) * m
    o_ref[...] = _layer_norm(z, lng_ref[...], lnb_ref[...])


def _combine(pos, y, x, wts_t, g_f, ln_g, ln_b, seq):
    t, d = x.shape
    tm = TM_COMBINE
    tiles_per_seq = seq // tm
    grid_spec = pltpu.PrefetchScalarGridSpec(
        num_scalar_prefetch=1,
        grid=(t // tm,),
        in_specs=[
            pl.BlockSpec(memory_space=pl.ANY),
            pl.BlockSpec((tm, d), lambda i, pos_: (i, 0)),
            pl.BlockSpec((tm, TOP_K), lambda i, pos_: (i, 0)),
            pl.BlockSpec((1, 1, d), lambda i, pos_: (i // tiles_per_seq, 0, 0)),
            pl.BlockSpec((1, d), lambda i, pos_: (0, 0)),
            pl.BlockSpec((1, d), lambda i, pos_: (0, 0)),
        ],
        out_specs=pl.BlockSpec((tm, d), lambda i, pos_: (i, 0)),
        scratch_shapes=[pltpu.VMEM((2, TOP_K, tm * SUBLANES, LANES), F32),
                        pltpu.SemaphoreType.DMA((2,))],
    )
    return pl.pallas_call(
        functools.partial(_combine_kernel, n_tok=t),
        out_shape=jax.ShapeDtypeStruct((t, d), F32),
        grid_spec=grid_spec,
        compiler_params=_params("arbitrary"),
        name="combine",
    )(pos, y, x, wts_t, g_f, ln_g, ln_b)


def kernel(x, c, ada_w, ada_b, ln_attn_g, ln_attn_b, ln_ffn_g, ln_ffn_b, fox_w_in, fox_b_f, fox_w_o,
           kv_ada_w, kv_ada_b, kv_w, diff_w_q, diff_lambda, diff_norm_g, diff_w_o, router_w, router_b,
           moe_w_gate, moe_w_up, moe_w_down):
    bsz, seq, d = x.shape
    t = bsz * seq
    rows_pad = TOP_K * t + N_EXPERTS * TM_EXPERT

    mod = _ada(c, ada_w, ada_b, 1536)
    kv_mod = _ada(c, kv_ada_w[None], kv_ada_b[None], 1024)[0]

    def mod_part(l, k):
        return mod[l, :, k * d:(k + 1) * d].reshape(bsz, 1, d)

    router_wt = router_w.T
    router_bc = router_b.reshape(N_EXPERTS, 1)
    xt = x.reshape(t, d)

    k_sh = v_sh_t = None
    alibi_kb = _alibi_key_bias(seq)
    for l in range(DEPTH):
        sh_a, sc_a, g_a, sh_f, sc_f, g_f = (mod_part(l, k) for k in range(6))
        if l < N_A:
            w_in = fox_w_in[l]
            w_qt = w_in[:, :d].T.astype(BF16)
            w_k = w_in[:, d:2 * d].astype(BF16)
            w_vt = w_in[:, 2 * d:3 * d].T.astype(BF16)
            w_f = jnp.pad(w_in[:, 3 * d:], ((0, 0), (0, HEAD_BLOCK - FOX_HEADS))).astype(BF16)
            b_f = jnp.pad(fox_b_f[l], (0, HEAD_BLOCK - FOX_HEADS)).reshape(1, HEAD_BLOCK)
            k_arr, q_t, v_t, key_bias, cum_t = _modproj(xt, sh_a, sc_a, [w_k], [w_qt, w_vt], seq,
                                                        forget=(w_f, b_f))
            o = _attention(
                "fox", q_t, k_arr, 0, key_bias,
                pl.BlockSpec((seq, HEAD_BLOCK), lambda b, h: (b, h)),
                v_t, 0, (cum_t.reshape(N_HEAD_BLOCKS, 2, t),),
                [pl.BlockSpec((1, 2, seq), lambda b, h: (h, 0, b))],
                bsz, seq)
            w_o = fox_w_o[l].astype(BF16)
        else:
            j = l - N_A
            if l == N_A:
                k_sh, v_sh_t = _modproj(xt, kv_mod[:, :d].reshape(bsz, 1, d),
                                        kv_mod[:, d:].reshape(bsz, 1, d),
                                        [kv_w[:, :d].astype(BF16)], [kv_w[:, d:].T.astype(BF16)], seq)
            (q_t,) = _modproj(xt, sh_a, sc_a, [], [diff_w_q[j].T.astype(BF16)], seq)
            lam_init = 0.8 - 0.6 * math.exp(-0.3 * l)
            o = _attention(
                "diff", q_t, k_sh, 0, alibi_kb,
                pl.BlockSpec((seq, HEAD_BLOCK), lambda b, h: (0, 0)),
                v_sh_t, 0,
                (diff_lambda[j][None], diff_norm_g[j].reshape(1, HEAD_BLOCK)),
                [pl.BlockSpec((1, 4, HALF), lambda b, h: (0, 0, 0)),
                 pl.BlockSpec((1, HEAD_BLOCK), lambda b, h: (0, 0))],
                bsz, seq, lam_init=lam_init)
            w_o = diff_w_o[j].astype(BF16)

        xt, h, eidx, wts, rank, counts = _post_attn(
            o, xt, w_o, g_a, ln_attn_g[l].reshape(1, d), ln_attn_b[l].reshape(1, d), sh_f, sc_f,
            router_wt, router_bc, seq)
        pos, pad_lo, pad_hi, tile_expert, n_valid = _routing_tables(
            eidx, rank, counts[:, 0].astype(jnp.int32), rows_pad)
        pos8 = pos * SUBLANES
        hs = _dispatch(h, pos8, pad_lo, pad_hi, rows_pad)
        y = _experts(hs, tile_expert, n_valid, moe_w_gate[l].astype(BF16),
                     moe_w_up[l].astype(BF16), moe_w_down[l].astype(BF16))
        xt = _combine(pos8, y, xt, wts.T, g_f, ln_ffn_g[l].reshape(1, d),
                      ln_ffn_b[l].reshape(1, d), seq)
    return xt.reshape(bsz, seq, d)
```

```python
import functools
import math

import jax
import jax.numpy as jnp
import numpy as np
from jax import lax
from jax.experimental import pallas as pl
from jax.experimental.pallas import tpu as pltpu

F32 = jnp.float32
BF16 = jnp.bfloat16
HIGHEST = lax.Precision.HIGHEST

D_MODEL = 1024
DEPTH = 4
N_A = DEPTH // 2
HEAD_BLOCK = 128
N_HEAD_BLOCKS = D_MODEL // HEAD_BLOCK
HALF = HEAD_BLOCK // 2
FOX_HEADS = 16
DIFF_HEADS = 8
N_EXPERTS = 16
N_GROUPS = 4
EXPERTS_PER_GROUP = N_EXPERTS // N_GROUPS
TOP_K = 2
D_EXPERT = 512
DEEPNORM_ALPHA = (2 * DEPTH) ** 0.25
LN_EPS = 1e-5
NEG_INF = -1e30
LOG2E = math.log2(math.e)

VMEM_LIMIT = 56 * 1024 * 1024

TM_PROJ = 512
TN_PROJ = 256
TQ = 512
TK = 512
DIAG_STRIPS = 2
BLOCKS_PER_TRIP = 4
LOOKAHEAD = 3
BIAS_ROWS = 16
ONES_ROWS = 16
TM_EXPERT = 512
TM_COMBINE = 256
TM_DISPATCH = 1024

NT_DIMS = (((1,), (1,)), ((), ()))


def _params(*sem, flags=None):
    return pltpu.CompilerParams(dimension_semantics=sem, vmem_limit_bytes=VMEM_LIMIT, flags=flags)


def _split3(v):
    hi = v.astype(BF16).astype(F32)
    r = v - hi
    mid = r.astype(BF16).astype(F32)
    lo = (r - mid).astype(BF16).astype(F32)
    return hi, mid, lo


def _ada_kernel(c_ref, w_ref, b_ref, o_ref):
    c = c_ref[...]
    sc = c * (1.0 / (1.0 + jnp.exp(-c)))
    o_ref[0] = jnp.dot(sc, w_ref[0], precision=HIGHEST, preferred_element_type=F32) + b_ref[0]


def _ada(c, w, b, tn):
    n_layers, d, n = w.shape
    bsz = c.shape[0]
    return pl.pallas_call(
        _ada_kernel,
        out_shape=jax.ShapeDtypeStruct((n_layers, bsz, n), F32),
        grid=(n_layers, n // tn),
        in_specs=[
            pl.BlockSpec((bsz, d), lambda l, j: (0, 0)),
            pl.BlockSpec((1, d, tn), lambda l, j: (l, 0, j)),
            pl.BlockSpec((1, 1, tn), lambda l, j: (l, 0, j)),
        ],
        out_specs=pl.BlockSpec((1, bsz, tn), lambda l, j: (l, 0, j)),
        compiler_params=_params("parallel", "parallel"),
        name="ada_params",
    )(c, w, b.reshape(n_layers, 1, n))


def _modproj_kernel(*refs, n_std, n_t, with_forget, tiles_per_seq):
    x_ref, sh_ref, sc_ref = refs[:3]
    pos = 3
    wstd = refs[pos:pos + n_std]; pos += n_std
    wt = refs[pos:pos + n_t]; pos += n_t
    if with_forget:
        wf_ref, bf_ref, sel_ref, ones_ref = refs[pos:pos + 4]; pos += 4
    ostd = refs[pos:pos + n_std]; pos += n_std
    ot = refs[pos:pos + n_t]; pos += n_t
    if with_forget:
        kb_ref, cumt_ref, carry_ref = refs[pos:pos + 3]

    hb = (x_ref[...] * (1.0 + sc_ref[0]) + sh_ref[0]).astype(BF16)
    for w_ref, o_ref in zip(wstd, ostd):
        for j in range(w_ref.shape[1] // TN_PROJ):
            cols = slice(j * TN_PROJ, (j + 1) * TN_PROJ)
            o_ref[:, cols] = jnp.dot(hb, w_ref[:, cols], preferred_element_type=F32).astype(o_ref.dtype)
    for w_ref, o_ref in zip(wt, ot):
        for j in range(w_ref.shape[0] // TN_PROJ):
            rows = slice(j * TN_PROJ, (j + 1) * TN_PROJ)
            o_ref[rows, :] = lax.dot_general(w_ref[rows, :], hb, NT_DIMS,
                                             preferred_element_type=F32).astype(o_ref.dtype)

    if with_forget:
        @pl.when(pl.program_id(0) % tiles_per_seq == 0)
        def _():
            carry_ref[...] = jnp.zeros_like(carry_ref)

        z = jnp.dot(hb, wf_ref[...], preferred_element_type=F32) + bf_ref[...]
        logf = jnp.minimum(z, 0.0) - jnp.log(1.0 + jnp.exp(-jnp.abs(z)))
        tm = z.shape[0]
        lower = (lax.broadcasted_iota(jnp.int32, (tm, tm), 0)
                 >= lax.broadcasted_iota(jnp.int32, (tm, tm), 1)).astype(F32)
        cum = jnp.dot(lower, logf, precision=HIGHEST, preferred_element_type=F32) + carry_ref[...]
        cumt_ref[...] = cum.T[:FOX_HEADS, :]
        carry_ref[...] = cum[tm - 1:tm, :]
        terms = jnp.concatenate([v.astype(BF16) for v in _split3((-LOG2E) * cum)], axis=1)
        for j in range(sel_ref.shape[1] // TN_PROJ):
            cols = slice(j * TN_PROJ, (j + 1) * TN_PROJ)
            kb_ref[:, cols] = (jnp.dot(terms, sel_ref[:, cols], preferred_element_type=F32)
                               + ones_ref[:, cols]).astype(kb_ref.dtype)


def _fox_bias_selection():
    sel = np.zeros((3 * HEAD_BLOCK, D_MODEL), np.float32)
    ones = np.zeros((1, D_MODEL), np.float32)
    for hp in range(N_HEAD_BLOCKS):
        for a in range(2):
            for term in range(3):
                sel[term * HEAD_BLOCK + 2 * hp + a, hp * HEAD_BLOCK + 6 * a + term] = 1.0
            ones[0, hp * HEAD_BLOCK + 6 * a + 3:hp * HEAD_BLOCK + 6 * a + 6] = 1.0
    return jnp.asarray(sel, BF16), jnp.asarray(ones, F32)


def _modproj(x, shift, scale, w_std, w_t, seq, forget=None):
    t, d = x.shape
    tm = TM_PROJ
    tiles_per_seq = seq // tm
    const = lambda i: (0, 0)
    mod_spec = pl.BlockSpec((1, 1, d), lambda i: (i // tiles_per_seq, 0, 0))
    in_specs = [pl.BlockSpec((tm, d), lambda i: (i, 0)), mod_spec, mod_spec]
    in_specs += [pl.BlockSpec(w.shape, const) for w in w_std]
    in_specs += [pl.BlockSpec(w.shape, const) for w in w_t]
    out_shape = [jax.ShapeDtypeStruct((t, w.shape[1]), BF16) for w in w_std]
    out_shape += [jax.ShapeDtypeStruct((w.shape[0], t), BF16) for w in w_t]
    out_specs = [pl.BlockSpec((tm, w.shape[1]), lambda i: (i, 0)) for w in w_std]
    out_specs += [pl.BlockSpec((w.shape[0], tm), lambda i: (0, i)) for w in w_t]
    args = [x, shift, scale, *w_std, *w_t]
    scratch = []
    if forget is not None:
        wf, bf = forget
        sel, ones = _fox_bias_selection()
        in_specs += [pl.BlockSpec(a.shape, const) for a in (wf, bf, sel, ones)]
        args += [wf, bf, sel, ones]
        out_shape += [jax.ShapeDtypeStruct((t, d), BF16), jax.ShapeDtypeStruct((FOX_HEADS, t), F32)]
        out_specs += [pl.BlockSpec((tm, d), lambda i: (i, 0)),
                      pl.BlockSpec((FOX_HEADS, tm), lambda i: (0, i))]
        scratch = [pltpu.VMEM((1, HEAD_BLOCK), F32)]
    return pl.pallas_call(
        functools.partial(_modproj_kernel, n_std=len(w_std), n_t=len(w_t),
                          with_forget=forget is not None, tiles_per_seq=tiles_per_seq),
        out_shape=tuple(out_shape),
        grid=(t // tm,),
        in_specs=in_specs,
        out_specs=tuple(out_specs),
        scratch_shapes=scratch,
        compiler_params=_params("arbitrary"),
        name="modproj_fox" if forget is not None else "modproj",
    )(*args)


def _bias_rows(entries, tq):
    rid = lax.broadcasted_iota(jnp.int32, (BIAS_ROWS, tq), 0)
    out = jnp.zeros((BIAS_ROWS, tq), F32)
    for r, v in entries.items():
        out = jnp.where(rid == r, v, out)
    return out


def _attn_kernel(*refs, mode, lam_init, n_heads):
    refs = list(refs)
    q_at, o_at = 0, (5 if mode == "fox" else 6)
    q_ref, o_ref = refs[q_at], refs[o_at]

    def query_tile(i, carry):
        rows = pl.ds(pl.multiple_of(i * TQ, TQ), TQ)
        tile_refs = list(refs)
        tile_refs[q_at] = q_ref.at[:, rows]
        tile_refs[o_at] = o_ref.at[rows, :]
        if mode == "fox":
            tile_refs[4] = refs[4].at[:, :, rows]
        _attn_query_tile(i, *tile_refs, mode=mode, lam_init=lam_init, n_heads=n_heads)
        return carry

    lax.fori_loop(0, q_ref.shape[1] // TQ, query_tile, 0)


def _attn_query_tile(i, *refs, mode, lam_init, n_heads):
    if mode == "fox":
        q_ref, k_ref, kb_ref, v_ref, cumt_ref, o_ref, m_ref, acc_ref = refs
    else:
        q_ref, k_ref, kb_ref, v_ref, lam_ref, g_ref, o_ref, m_ref, acc_ref = refs
    hp = pl.program_id(1)
    tq = q_ref.shape[1]

    qs = q_ref[...] * jnp.asarray(HALF ** -0.5 * LOG2E, BF16)
    zeros_half = jnp.zeros((HALF, tq), BF16)
    q_tops = (jnp.concatenate([qs[:HALF], zeros_half], axis=0),
              jnp.concatenate([zeros_half, qs[HALF:]], axis=0))
    pad = jnp.zeros((HEAD_BLOCK - BIAS_ROWS, tq), BF16)
    if mode == "fox":
        q_bias = []
        for a in range(2):
            c = _split3(cumt_ref[0, a:a + 1, :] * LOG2E)
            base = 6 * a
            rows = {base: 1.0, base + 1: 1.0, base + 2: 1.0,
                    base + 3: c[0], base + 4: c[1], base + 5: c[2]}
            q_bias.append(_bias_rows(rows, tq).astype(BF16))
    else:
        slope = jnp.exp2(jnp.full((1, 1), -8.0, F32) * (hp + 1).astype(F32) / n_heads)
        l3 = _split3(slope * LOG2E)
        qpos = (i * tq + lax.broadcasted_iota(jnp.int32, (1, tq), 1)).astype(F32)
        t3 = _split3(-(slope * LOG2E) * qpos)
        rows = {0: l3[0], 1: l3[0], 2: l3[1], 3: l3[1], 4: l3[2], 5: l3[2],
                6: t3[0], 7: t3[1], 8: t3[2]}
        q_bias = [_bias_rows(rows, tq).astype(BF16)] * 2
    q_aug = tuple(jnp.concatenate([q_tops[a], q_bias[a], pad], axis=0) for a in range(2))

    m_ref[...] = jnp.full_like(m_ref, NEG_INF)
    acc_ref[...] = jnp.zeros_like(acc_ref)
    ones_rows = jnp.ones((ONES_ROWS, TK), BF16)

    def key_slice(k0, nk):
        return pl.ds(pl.multiple_of(k0, nk), nk)

    def scores(a, k0, nk, q0, nq):
        ks = key_slice(k0, nk)
        k_aug = jnp.concatenate([k_ref[ks, :], kb_ref[ks, :]], axis=1)
        return jnp.dot(k_aug, q_aug[a][:, q0:q0 + nq], preferred_element_type=F32)

    v_rows = [slice(a * HALF, (a + 1) * HALF) if mode == "fox" else slice(0, HEAD_BLOCK)
              for a in range(2)]
    n_v = v_rows[0].stop - v_rows[0].start

    def accumulate(a, k0, nk, q0, nq, masked, s):
        ql = slice(q0, q0 + nq)
        v_aug = jnp.concatenate([v_ref[v_rows[a], key_slice(k0, nk)], ones_rows[:, :nk]], axis=0)
        if masked:
            keep = ((k0 + lax.broadcasted_iota(jnp.int32, s.shape, 0))
                    <= (i * tq + q0 + lax.broadcasted_iota(jnp.int32, s.shape, 1)))
            s = jnp.where(keep, s, NEG_INF)
        m_old = m_ref[a, :, ql]
        m_new = jnp.maximum(m_old, jnp.max(s, axis=0, keepdims=True))
        alpha = jnp.exp2(m_old - m_new)
        p = jnp.exp2(s - m_new).astype(BF16)
        acc_ref[a, :, ql] = alpha * acc_ref[a, :, ql] + jnp.dot(v_aug, p, preferred_element_type=F32)
        m_ref[a, :, ql] = m_new

    def full_block(j):
        return [(a, j * TK, TK, 0, tq, False) for a in range(2)]

    def diagonal_block(j):
        strip = TK // DIAG_STRIPS
        return [(a, j * TK + c * strip, strip, c * strip, tq - c * strip, True)
                for c in range(DIAG_STRIPS) for a in range(2)]

    def process(work):
        pending = [scores(*w[:5]) for w in work[:LOOKAHEAD]]
        for n, w in enumerate(work):
            s = pending.pop(0)
            if n + LOOKAHEAD < len(work):
                pending.append(scores(*work[n + LOOKAHEAD][:5]))
            accumulate(*w, s)

    n_full = (i * tq) // TK

    def loop_body(p, carry):
        process(sum((full_block(BLOCKS_PER_TRIP * p + d) for d in range(BLOCKS_PER_TRIP)), []))
        return carry

    lax.fori_loop(0, n_full // BLOCKS_PER_TRIP, loop_body, 0)

    for rem in range(BLOCKS_PER_TRIP):
        @pl.when(n_full % BLOCKS_PER_TRIP == rem)
        def _(rem=rem):
            process(sum((full_block(n_full - rem + d) for d in range(rem)), [])
                    + diagonal_block(n_full))

    inv = [1.0 / acc_ref[a, n_v:n_v + 1, :] for a in range(2)]
    if mode == "fox":
        o_t = jnp.concatenate([acc_ref[a, :n_v, :] * inv[a] for a in range(2)], axis=0)
        o_ref[...] = o_t.T.astype(o_ref.dtype)
    else:
        lp = lam_ref[0]
        lam = (jnp.exp(jnp.sum(lp[0:1] * lp[1:2], axis=-1, keepdims=True))
               - jnp.exp(jnp.sum(lp[2:3] * lp[3:4], axis=-1, keepdims=True)) + lam_init)
        o_t = acc_ref[0, :HEAD_BLOCK, :] * inv[0] - lam * (acc_ref[1, :HEAD_BLOCK, :] * inv[1])
        o = o_t.T
        o = o * lax.rsqrt(jnp.mean(o * o, axis=-1, keepdims=True) + LN_EPS)
        o_ref[...] = (o * g_ref[...] * (1.0 - lam_init)).astype(o_ref.dtype)


def _attention(mode, q_t, k_arr, k_off, kb, kb_spec, v_t, v_off, extras, extra_specs, batch, seq,
               lam_init=0.0, n_heads=DIFF_HEADS):
    assert TQ == TK, "one masked diagonal block per query tile"
    t = batch * seq
    acc_rows = (HALF if mode == "fox" else HEAD_BLOCK) + ONES_ROWS
    in_specs = [
        pl.BlockSpec((HEAD_BLOCK, seq), lambda b, h: (h, b)),
        pl.BlockSpec((seq, HEAD_BLOCK), lambda b, h: (b, k_off + h)),
        kb_spec,
        pl.BlockSpec((HEAD_BLOCK, seq), lambda b, h: (v_off + h, b)),
    ] + extra_specs
    return pl.pallas_call(
        functools.partial(_attn_kernel, mode=mode, lam_init=lam_init, n_heads=n_heads),
        out_shape=jax.ShapeDtypeStruct((t, D_MODEL), BF16),
        grid=(batch, N_HEAD_BLOCKS),
        in_specs=in_specs,
        out_specs=pl.BlockSpec((seq, HEAD_BLOCK), lambda b, h: (b, h)),
        scratch_shapes=[pltpu.VMEM((2, 1, TQ), F32),
                        pltpu.VMEM((2, acc_rows, TQ), F32)],
        compiler_params=_params("parallel", "parallel"),
        name="attn_" + mode,
    )(q_t, k_arr, kb, v_t, *extras)


def _bias_lanes(cols):
    lanes = jnp.stack(cols, axis=-1)
    pad = [(0, 0)] * (lanes.ndim - 1) + [(0, HEAD_BLOCK - lanes.shape[-1])]
    return jnp.pad(lanes, pad).astype(BF16)


def _alibi_key_bias(seq):
    kpos = jnp.arange(seq, dtype=jnp.int32)
    hi = ((kpos // 64) * 64).astype(F32)
    lo = (kpos % 64).astype(F32)
    one = jnp.ones((seq,), F32)
    return _bias_lanes([hi, lo, hi, lo, hi, lo, one, one, one])


SUBLANES = 8
LANES = 128
N_DMA_PRIORITIES = 2


def _store_token_tiled(ref, value):
    n = value.shape[0]
    for s in range(SUBLANES):
        ref[pl.ds(s, n, stride=SUBLANES), :] = value[:, s * LANES:(s + 1) * LANES]


def _load_token_tiled(ref, n):
    return jnp.concatenate([ref[pl.ds(s, n, stride=SUBLANES), :] for s in range(SUBLANES)], axis=1)


def _layer_norm(z, g, b):
    mu = jnp.mean(z, axis=-1, keepdims=True)
    zc = z - mu
    var = jnp.mean(zc * zc, axis=-1, keepdims=True)
    return zc * lax.rsqrt(var + LN_EPS) * g + b


def _argmax_first(vals):
    best, idx = vals[0], jnp.zeros(vals[0].shape, jnp.int32)
    for j in range(1, len(vals)):
        better = vals[j] > best
        idx = jnp.where(better, j, idx)
        best = jnp.where(better, vals[j], best)
    return idx


def _select(vals, idx):
    out = vals[0]
    for j in range(1, len(vals)):
        out = jnp.where(idx == j, vals[j], out)
    return out


def _post_attn_kernel(o_ref, x_ref, wo_ref, ga_ref, lng_ref, lnb_ref, shf_ref, scf_ref,
                      rwt_ref, rb_ref, upper_ref, xo_ref, h_ref, eidx_ref, wts_ref, rank_ref,
                      counts_ref, count_ref):
    y = jnp.dot(o_ref[...], wo_ref[...], preferred_element_type=F32)
    z = DEEPNORM_ALPHA * x_ref[...] + (1.0 + ga_ref[0]) * y
    xn = _layer_norm(z, lng_ref[...], lnb_ref[...])
    xo_ref[...] = xn
    h = xn * (1.0 + scf_ref[0]) + shf_ref[0]
    _store_token_tiled(h_ref, h)

    logits = lax.dot_general(rwt_ref[...], h, NT_DIMS, precision=HIGHEST,
                             preferred_element_type=F32)
    rows = [logits[e:e + 1, :] for e in range(N_EXPERTS)]
    mx = functools.reduce(jnp.maximum, rows)
    ex = [jnp.exp(r - mx) for r in rows]
    den = functools.reduce(lambda a, b: a + b, ex)
    probs = [e_ / den for e_ in ex]
    sel = [probs[e] + rb_ref[e:e + 1, :] for e in range(N_EXPERTS)]

    grp_scores = []
    for g in range(N_GROUPS):
        v = sel[g * EXPERTS_PER_GROUP:(g + 1) * EXPERTS_PER_GROUP]
        pair_sums = [v[a] + v[b] for a in range(EXPERTS_PER_GROUP) for b in range(a + 1, EXPERTS_PER_GROUP)]
        grp_scores.append(functools.reduce(jnp.maximum, pair_sums))
    g_idx = _argmax_first(grp_scores)

    in_sel = [_select([sel[g * EXPERTS_PER_GROUP + j] for g in range(N_GROUPS)], g_idx)
              for j in range(EXPERTS_PER_GROUP)]
    in_prob = [_select([probs[g * EXPERTS_PER_GROUP + j] for g in range(N_GROUPS)], g_idx)
               for j in range(EXPERTS_PER_GROUP)]
    i1 = _argmax_first(in_sel)
    b2 = jnp.full(in_sel[0].shape, -jnp.inf, F32)
    i2 = jnp.zeros(in_sel[0].shape, jnp.int32)
    for j in range(EXPERTS_PER_GROUP):
        cand = (i1 != j) & (in_sel[j] > b2)
        i2 = jnp.where(cand, j, i2)
        b2 = jnp.where(cand, in_sel[j], b2)
    p1 = _select(in_prob, i1)
    p2 = _select(in_prob, i2)
    tot = p1 + p2
    experts = (g_idx * EXPERTS_PER_GROUP + i1, g_idx * EXPERTS_PER_GROUP + i2)
    wts_ref[0:1, :] = p1 / tot
    wts_ref[1:2, :] = p2 / tot

    @pl.when(pl.program_id(0) == 0)
    def _():
        count_ref[...] = jnp.zeros_like(count_ref)

    tm = experts[0].shape[1]
    eid = lax.broadcasted_iota(jnp.int32, (N_EXPERTS, tm), 0)
    hit = [eid == e for e in experts]
    chosen = jnp.where(hit[0] | hit[1], 1.0, 0.0)
    before = jnp.dot(chosen.astype(BF16), upper_ref[...], preferred_element_type=F32) + count_ref[...]
    for k in range(TOP_K):
        eidx_ref[k:k + 1, :] = experts[k]
        rank_ref[k:k + 1, :] = jnp.sum(jnp.where(hit[k], before, 0.0), axis=0,
                                       keepdims=True).astype(jnp.int32)
    count_ref[...] += jnp.sum(chosen, axis=1, keepdims=True)
    counts_ref[...] = jnp.broadcast_to(count_ref[...], counts_ref.shape)


def _post_attn(o, x, wo, g_a, ln_g, ln_b, sh_f, sc_f, router_wt, router_b, seq):
    t, d = x.shape
    tm = TM_PROJ
    tiles_per_seq = seq // tm
    mod_spec = pl.BlockSpec((1, 1, d), lambda i: (i // tiles_per_seq, 0, 0))
    vec_spec = pl.BlockSpec((1, d), lambda i: (0, 0))
    tok_spec = pl.BlockSpec((tm, d), lambda i: (i, 0))
    route_spec = pl.BlockSpec((TOP_K, tm), lambda i: (0, i))
    upper = jnp.asarray(np.triu(np.ones((tm, tm), np.float32), k=1), BF16)
    return pl.pallas_call(
        _post_attn_kernel,
        out_shape=(jax.ShapeDtypeStruct((t, d), F32), jax.ShapeDtypeStruct((t * SUBLANES, LANES), F32),
                   jax.ShapeDtypeStruct((TOP_K, t), jnp.int32), jax.ShapeDtypeStruct((TOP_K, t), F32),
                   jax.ShapeDtypeStruct((TOP_K, t), jnp.int32),
                   jax.ShapeDtypeStruct((N_EXPERTS, LANES), F32)),
        grid=(t // tm,),
        in_specs=[
            tok_spec, tok_spec,
            pl.BlockSpec((d, d), lambda i: (0, 0)),
            mod_spec, vec_spec, vec_spec, mod_spec, mod_spec,
            pl.BlockSpec((N_EXPERTS, d), lambda i: (0, 0)),
            pl.BlockSpec((N_EXPERTS, 1), lambda i: (0, 0)),
            pl.BlockSpec((tm, tm), lambda i: (0, 0)),
        ],
        out_specs=(tok_spec, pl.BlockSpec((tm * SUBLANES, LANES), lambda i: (i, 0)),
                   route_spec, route_spec, route_spec,
                   pl.BlockSpec((N_EXPERTS, LANES), lambda i: (0, 0))),
        scratch_shapes=[pltpu.VMEM((N_EXPERTS, 1), F32)],
        compiler_params=_params("arbitrary"),
        name="post_attn",
    )(o, x, wo, g_a, ln_g, ln_b, sh_f, sc_f, router_wt, router_b, upper)


def _routing_tables(eidx, rank, counts, rows_pad):
    t = eidx.shape[1]
    padded = ((counts + TM_EXPERT - 1) // TM_EXPERT) * TM_EXPERT
    ends = jnp.cumsum(padded)
    starts = ends - padded
    start_of = sum(jnp.where(eidx == e, starts[e], 0) for e in range(N_EXPERTS))
    pos = (start_of + rank).reshape(-1)
    n_tiles = rows_pad // TM_EXPERT
    tile_start = jnp.arange(n_tiles, dtype=jnp.int32) * TM_EXPERT
    tile_expert = jnp.minimum(jnp.sum((tile_start[:, None] >= ends[None, :]).astype(jnp.int32), axis=1),
                              N_EXPERTS - 1).astype(jnp.int32)
    n_valid = (ends[-1] // TM_EXPERT).astype(jnp.int32).reshape(1)
    pad_lo = (starts + counts).astype(jnp.int32)
    pad_hi = jnp.concatenate([ends[:-1], jnp.full((1,), rows_pad, ends.dtype)]).astype(jnp.int32)
    return pos.astype(jnp.int32), pad_lo, pad_hi, tile_expert, n_valid


def _row_tile(ref, row8):
    return ref.at[pl.ds(pl.multiple_of(row8, SUBLANES), SUBLANES)]


def _dispatch_kernel(pos8_ref, pad_lo_ref, pad_hi_ref, h_ref, hs_hbm, zero_ref, sem, pad_sem, *, n_tok):
    i = pl.program_id(0)
    tm = h_ref.shape[0] // SUBLANES
    base = i * tm

    def body(r, carry):
        for k in range(TOP_K):
            pltpu.make_async_copy(_row_tile(h_ref, r * SUBLANES),
                                  _row_tile(hs_hbm, pos8_ref[k * n_tok + base + r]),
                                  sem).start(priority=k % N_DMA_PRIORITIES)
        return carry

    lax.fori_loop(0, tm, body, 0, unroll=8)

    @pl.when(i == 0)
    def _():
        zero_ref[...] = jnp.zeros_like(zero_ref)
        for e in range(N_EXPERTS):
            lo, hi = pad_lo_ref[e], pad_hi_ref[e]

            def start(r, carry):
                pltpu.make_async_copy(zero_ref, _row_tile(hs_hbm, r * SUBLANES), pad_sem).start()
                return carry

            def wait(r, carry):
                pltpu.make_async_copy(zero_ref, _row_tile(hs_hbm, r * SUBLANES), pad_sem).wait()
                return carry

            lax.fori_loop(lo, hi, start, 0)
            lax.fori_loop(lo, hi, wait, 0)

    for k in range(TOP_K):
        pltpu.make_async_copy(h_ref, hs_hbm.at[pl.ds(0, tm * SUBLANES)], sem).wait()


def _dispatch(h_tiled, pos8, pad_lo, pad_hi, rows_pad):
    t = h_tiled.shape[0] // SUBLANES
    tm = TM_DISPATCH
    grid_spec = pltpu.PrefetchScalarGridSpec(
        num_scalar_prefetch=3,
        grid=(t // tm,),
        in_specs=[pl.BlockSpec((tm * SUBLANES, LANES), lambda i, *_: (i, 0))],
        out_specs=pl.BlockSpec(memory_space=pl.ANY),
        scratch_shapes=[pltpu.VMEM((SUBLANES, LANES), F32), pltpu.SemaphoreType.DMA(()),
                        pltpu.SemaphoreType.DMA(())],
    )
    return pl.pallas_call(
        functools.partial(_dispatch_kernel, n_tok=t),
        out_shape=jax.ShapeDtypeStruct((rows_pad * SUBLANES, LANES), F32),
        grid_spec=grid_spec,
        compiler_params=_params("arbitrary"),
        name="dispatch",
    )(pos8, pad_lo, pad_hi, h_tiled)


def _expert_kernel(te_ref, nv_ref, hs_ref, wg_ref, wu_ref, wd_ref, y_ref, wg_bf, wu_bf, wd_bf):
    i = pl.program_id(0)
    nv = nv_ref[0]

    @pl.when((i < nv) & ((i == 0) | (te_ref[i] != te_ref[jnp.maximum(i - 1, 0)])))
    def _():
        wg_bf[...] = wg_ref[0, 0].astype(BF16)
        wu_bf[...] = wu_ref[0, 0].astype(BF16)
        wd_bf[...] = wd_ref[0, 0].astype(BF16)

    @pl.when(i < nv)
    def _():
        hb = _load_token_tiled(hs_ref, TM_EXPERT).astype(BF16)
        g = jnp.dot(hb, wg_bf[...], preferred_element_type=F32)
        u = jnp.dot(hb, wu_bf[...], preferred_element_type=F32)
        a = g * (1.0 / (1.0 + jnp.exp(-g))) * u
        _store_token_tiled(y_ref, jnp.dot(a.astype(BF16), wd_bf[...], preferred_element_type=F32))

    @pl.when(i >= nv)
    def _():
        y_ref[...] = jnp.zeros_like(y_ref)


def _experts(hs, tile_expert, n_valid, wg, wu, wd, layer):
    d = wg.shape[2]
    rows_pad = hs.shape[0] // SUBLANES
    n_tiles = rows_pad // TM_EXPERT
    grid_spec = pltpu.PrefetchScalarGridSpec(
        num_scalar_prefetch=2,
        grid=(n_tiles,),
        in_specs=[
            pl.BlockSpec((TM_EXPERT * SUBLANES, LANES), lambda i, te, nv: (i, 0)),
            pl.BlockSpec((1, 1, d, D_EXPERT), lambda i, te, nv: (layer, te[i], 0, 0)),
            pl.BlockSpec((1, 1, d, D_EXPERT), lambda i, te, nv: (layer, te[i], 0, 0)),
            pl.BlockSpec((1, 1, D_EXPERT, d), lambda i, te, nv: (layer, te[i], 0, 0)),
        ],
        out_specs=pl.BlockSpec((TM_EXPERT * SUBLANES, LANES), lambda i, te, nv: (i, 0)),
        scratch_shapes=[pltpu.VMEM((d, D_EXPERT), BF16), pltpu.VMEM((d, D_EXPERT), BF16),
                        pltpu.VMEM((D_EXPERT, d), BF16)],
    )
    return pl.pallas_call(
        _expert_kernel,
        out_shape=jax.ShapeDtypeStruct((rows_pad * SUBLANES, LANES), F32),
        grid_spec=grid_spec,
        compiler_params=_params("arbitrary"),
        name="experts",
    )(tile_expert, n_valid, hs, wg, wu, wd)


def _combine_kernel(pos_ref, y_hbm, x_ref, w_ref, gf_ref, lng_ref, lnb_ref, o_ref, buf, sem, *, n_tok):
    i = pl.program_id(0)
    n = pl.num_programs(0)
    slot = i % 2
    tm = x_ref.shape[0]

    def issue(tile, slot_):
        base = tile * tm

        def body(r, carry):
            for k in range(TOP_K):
                pltpu.make_async_copy(_row_tile(y_hbm, pos_ref[k * n_tok + base + r]),
                                      _row_tile(buf.at[slot_, k], r * SUBLANES),
                                      sem.at[slot_]).start(priority=k % N_DMA_PRIORITIES)
            return carry

        lax.fori_loop(0, tm, body, 0, unroll=8)

    @pl.when(i == 0)
    def _():
        issue(0, 0)

    @pl.when(i + 1 < n)
    def _():
        issue(i + 1, 1 - slot)

    for k in range(TOP_K):
        pltpu.make_async_copy(y_hbm.at[pl.ds(0, tm * SUBLANES)], buf.at[slot, k], sem.at[slot]).wait()
    w = w_ref[...]
    m = (w[:, 0:1] * _load_token_tiled(buf.at[slot, 0], tm)
         + w[:, 1:2] * _load_token_tiled(buf.at[slot, 1], tm))
    z = DEEPNORM_ALPHA * x_ref[...] + (1.0 + gf_ref[0]) * m
    o_ref[...] = _layer_norm(z, lng_ref[...], lnb_ref[...])


def _combine(pos, y, x, wts_t, g_f, ln_g, ln_b, seq):
    t, d = x.shape
    tm = TM_COMBINE
    tiles_per_seq = seq // tm
    grid_spec = pltpu.PrefetchScalarGridSpec(
        num_scalar_prefetch=1,
        grid=(t // tm,),
        in_specs=[
            pl.BlockSpec(memory_space=pl.ANY),
            pl.BlockSpec((tm, d), lambda i, pos_: (i, 0)),
            pl.BlockSpec((tm, TOP_K), lambda i, pos_: (i, 0)),
            pl.BlockSpec((1, 1, d), lambda i, pos_: (i // tiles_per_seq, 0, 0)),
            pl.BlockSpec((1, d), lambda i, pos_: (0, 0)),
            pl.BlockSpec((1, d), lambda i, pos_: (0, 0)),
        ],
        out_specs=pl.BlockSpec((tm, d), lambda i, pos_: (i, 0)),
        scratch_shapes=[pltpu.VMEM((2, TOP_K, tm * SUBLANES, LANES), F32),
                        pltpu.SemaphoreType.DMA((2,))],
    )
    return pl.pallas_call(
        functools.partial(_combine_kernel, n_tok=t),
        out_shape=jax.ShapeDtypeStruct((t, d), F32),
        grid_spec=grid_spec,
        compiler_params=_params("arbitrary"),
        name="combine",
    )(pos, y, x, wts_t, g_f, ln_g, ln_b)


def kernel(x, c, ada_w, ada_b, ln_attn_g, ln_attn_b, ln_ffn_g, ln_ffn_b, fox_w_in, fox_b_f, fox_w_o,
           kv_ada_w, kv_ada_b, kv_w, diff_w_q, diff_lambda, diff_norm_g, diff_w_o, router_w, router_b,
           moe_w_gate, moe_w_up, moe_w_down):
    bsz, seq, d = x.shape
    t = bsz * seq
    rows_pad = TOP_K * t + N_EXPERTS * TM_EXPERT

    mod = _ada(c, ada_w, ada_b, 1536)
    kv_mod = _ada(c, kv_ada_w[None], kv_ada_b[None], 1024)[0]

    def mod_part(l, k):
        return mod[l, :, k * d:(k + 1) * d].reshape(bsz, 1, d)

    router_wt = router_w.T
    router_bc = router_b.reshape(N_EXPERTS, 1)
    xt = x.reshape(t, d)

    k_sh = v_sh_t = None
    alibi_kb = _alibi_key_bias(seq)
    for l in range(DEPTH):
        sh_a, sc_a, g_a, sh_f, sc_f, g_f = (mod_part(l, k) for k in range(6))
        if l < N_A:
            w_in = fox_w_in[l]
            w_qt = w_in[:, :d].T.astype(BF16)
            w_k = w_in[:, d:2 * d].astype(BF16)
            w_vt = w_in[:, 2 * d:3 * d].T.astype(BF16)
            w_f = jnp.pad(w_in[:, 3 * d:], ((0, 0), (0, HEAD_BLOCK - FOX_HEADS))).astype(BF16)
            b_f = jnp.pad(fox_b_f[l], (0, HEAD_BLOCK - FOX_HEADS)).reshape(1, HEAD_BLOCK)
            k_arr, q_t, v_t, key_bias, cum_t = _modproj(xt, sh_a, sc_a, [w_k], [w_qt, w_vt], seq,
                                                        forget=(w_f, b_f))
            o = _attention(
                "fox", q_t, k_arr, 0, key_bias,
                pl.BlockSpec((seq, HEAD_BLOCK), lambda b, h: (b, h)),
                v_t, 0, (cum_t.reshape(N_HEAD_BLOCKS, 2, t),),
                [pl.BlockSpec((1, 2, seq), lambda b, h: (h, 0, b))],
                bsz, seq)
            w_o = fox_w_o[l].astype(BF16)
        else:
            j = l - N_A
            if l == N_A:
                k_sh, v_sh_t = _modproj(xt, kv_mod[:, :d].reshape(bsz, 1, d),
                                        kv_mod[:, d:].reshape(bsz, 1, d),
                                        [kv_w[:, :d].astype(BF16)], [kv_w[:, d:].T.astype(BF16)], seq)
            (q_t,) = _modproj(xt, sh_a, sc_a, [], [diff_w_q[j].T.astype(BF16)], seq)
            lam_init = 0.8 - 0.6 * math.exp(-0.3 * l)
            o = _attention(
                "diff", q_t, k_sh, 0, alibi_kb,
                pl.BlockSpec((seq, HEAD_BLOCK), lambda b, h: (0, 0)),
                v_sh_t, 0,
                (diff_lambda[j][None], diff_norm_g[j].reshape(1, HEAD_BLOCK)),
                [pl.BlockSpec((1, 4, HALF), lambda b, h: (0, 0, 0)),
                 pl.BlockSpec((1, HEAD_BLOCK), lambda b, h: (0, 0))],
                bsz, seq, lam_init=lam_init)
            w_o = diff_w_o[j].astype(BF16)

        xt, h, eidx, wts, rank, counts = _post_attn(
            o, xt, w_o, g_a, ln_attn_g[l].reshape(1, d), ln_attn_b[l].reshape(1, d), sh_f, sc_f,
            router_wt, router_bc, seq)
        pos, pad_lo, pad_hi, tile_expert, n_valid = _routing_tables(
            eidx, rank, counts[:, 0].astype(jnp.int32), rows_pad)
        pos8 = pos * SUBLANES
        hs = _dispatch(h, pos8, pad_lo, pad_hi, rows_pad)
        y = _experts(hs, tile_expert, n_valid, moe_w_gate, moe_w_up, moe_w_down, l)
        xt = _combine(pos8, y, xt, wts.T, g_f, ln_ffn_g[l].reshape(1, d),
                      ln_ffn_b[l].reshape(1, d), seq)
    return xt.reshape(bsz, seq, d)
```

```python
import functools
import math

import jax
import jax.numpy as jnp
import numpy as np
from jax import lax
from jax.experimental import pallas as pl
from jax.experimental.pallas import tpu as pltpu

F32 = jnp.float32
BF16 = jnp.bfloat16
HIGHEST = lax.Precision.HIGHEST

D_MODEL = 1024
DEPTH = 4
N_A = DEPTH // 2
HEAD_BLOCK = 128
N_HEAD_BLOCKS = D_MODEL // HEAD_BLOCK
HALF = HEAD_BLOCK // 2
FOX_HEADS = 16
DIFF_HEADS = 8
N_EXPERTS = 16
N_GROUPS = 4
EXPERTS_PER_GROUP = N_EXPERTS // N_GROUPS
TOP_K = 2
D_EXPERT = 512
DEEPNORM_ALPHA = (2 * DEPTH) ** 0.25
LN_EPS = 1e-5
NEG_INF = -1e30
LOG2E = math.log2(math.e)

VMEM_LIMIT = 56 * 1024 * 1024

TM_PROJ = 512
TN_PROJ = 256
TQ = 512
TK = 512
DIAG_STRIPS = 2
BLOCKS_PER_TRIP = 4
LOOKAHEAD = 3
BIAS_ROWS = 16
ONES_ROWS = 16
TM_EXPERT = 512
TM_COMBINE = 256
TM_DISPATCH = 1024

NT_DIMS = (((1,), (1,)), ((), ()))


def _params(*sem, flags=None):
    return pltpu.CompilerParams(dimension_semantics=sem, vmem_limit_bytes=VMEM_LIMIT, flags=flags)


def _split3(v):
    hi = v.astype(BF16).astype(F32)
    r = v - hi
    mid = r.astype(BF16).astype(F32)
    lo = (r - mid).astype(BF16).astype(F32)
    return hi, mid, lo


def _ada_kernel(c_ref, w_ref, b_ref, o_ref):
    c = c_ref[...]
    sc = c * (1.0 / (1.0 + jnp.exp(-c)))
    o_ref[0] = jnp.dot(sc, w_ref[0], precision=HIGHEST, preferred_element_type=F32) + b_ref[0]


def _ada(c, w, b, tn):
    n_layers, d, n = w.shape
    bsz = c.shape[0]
    return pl.pallas_call(
        _ada_kernel,
        out_shape=jax.ShapeDtypeStruct((n_layers, bsz, n), F32),
        grid=(n_layers, n // tn),
        in_specs=[
            pl.BlockSpec((bsz, d), lambda l, j: (0, 0)),
            pl.BlockSpec((1, d, tn), lambda l, j: (l, 0, j)),
            pl.BlockSpec((1, 1, tn), lambda l, j: (l, 0, j)),
        ],
        out_specs=pl.BlockSpec((1, bsz, tn), lambda l, j: (l, 0, j)),
        compiler_params=_params("parallel", "parallel"),
        name="ada_params",
    )(c, w, b.reshape(n_layers, 1, n))


def _modproj_kernel(*refs, n_std, n_t, with_forget, tiles_per_seq):
    x_ref, sh_ref, sc_ref = refs[:3]
    pos = 3
    wstd = refs[pos:pos + n_std]; pos += n_std
    wt = refs[pos:pos + n_t]; pos += n_t
    if with_forget:
        wf_ref, bf_ref, sel_ref, ones_ref, lower_ref = refs[pos:pos + 5]; pos += 5
    ostd = refs[pos:pos + n_std]; pos += n_std
    ot = refs[pos:pos + n_t]; pos += n_t
    if with_forget:
        kb_ref, cumt_ref, carry_ref = refs[pos:pos + 3]

    hb = (x_ref[...] * (1.0 + sc_ref[0]) + sh_ref[0]).astype(BF16)
    for w_ref, o_ref in zip(wstd, ostd):
        for j in range(w_ref.shape[1] // TN_PROJ):
            cols = slice(j * TN_PROJ, (j + 1) * TN_PROJ)
            o_ref[:, cols] = jnp.dot(hb, w_ref[:, cols], preferred_element_type=F32).astype(o_ref.dtype)
    for w_ref, o_ref in zip(wt, ot):
        for j in range(w_ref.shape[0] // TN_PROJ):
            rows = slice(j * TN_PROJ, (j + 1) * TN_PROJ)
            o_ref[rows, :] = lax.dot_general(w_ref[rows, :], hb, NT_DIMS,
                                             preferred_element_type=F32).astype(o_ref.dtype)

    if with_forget:
        @pl.when(pl.program_id(0) % tiles_per_seq == 0)
        def _():
            carry_ref[...] = jnp.zeros_like(carry_ref)

        z = jnp.dot(hb, wf_ref[...], preferred_element_type=F32) + bf_ref[...]
        logf = jnp.minimum(z, 0.0) - jnp.log(1.0 + jnp.exp(-jnp.abs(z)))
        tm = z.shape[0]
        cum = carry_ref[...]
        for term in _split3(logf):
            cum = cum + jnp.dot(lower_ref[...], term.astype(BF16), preferred_element_type=F32)
        cumt_ref[...] = cum.T[:FOX_HEADS, :]
        carry_ref[...] = cum[tm - 1:tm, :]
        terms = jnp.concatenate([v.astype(BF16) for v in _split3((-LOG2E) * cum)], axis=1)
        for j in range(sel_ref.shape[1] // TN_PROJ):
            cols = slice(j * TN_PROJ, (j + 1) * TN_PROJ)
            kb_ref[:, cols] = (jnp.dot(terms, sel_ref[:, cols], preferred_element_type=F32)
                               + ones_ref[:, cols]).astype(kb_ref.dtype)


def _fox_bias_selection():
    sel = np.zeros((3 * HEAD_BLOCK, D_MODEL), np.float32)
    ones = np.zeros((1, D_MODEL), np.float32)
    for hp in range(N_HEAD_BLOCKS):
        for a in range(2):
            for term in range(3):
                sel[term * HEAD_BLOCK + 2 * hp + a, hp * HEAD_BLOCK + 6 * a + term] = 1.0
            ones[0, hp * HEAD_BLOCK + 6 * a + 3:hp * HEAD_BLOCK + 6 * a + 6] = 1.0
    return jnp.asarray(sel, BF16), jnp.asarray(ones, F32)


def _modproj(x, shift, scale, w_std, w_t, seq, forget=None):
    t, d = x.shape
    tm = TM_PROJ
    tiles_per_seq = seq // tm
    const = lambda i: (0, 0)
    mod_spec = pl.BlockSpec((1, 1, d), lambda i: (i // tiles_per_seq, 0, 0))
    in_specs = [pl.BlockSpec((tm, d), lambda i: (i, 0)), mod_spec, mod_spec]
    in_specs += [pl.BlockSpec(w.shape, const) for w in w_std]
    in_specs += [pl.BlockSpec(w.shape, const) for w in w_t]
    out_shape = [jax.ShapeDtypeStruct((t, w.shape[1]), BF16) for w in w_std]
    out_shape += [jax.ShapeDtypeStruct((w.shape[0], t), BF16) for w in w_t]
    out_specs = [pl.BlockSpec((tm, w.shape[1]), lambda i: (i, 0)) for w in w_std]
    out_specs += [pl.BlockSpec((w.shape[0], tm), lambda i: (0, i)) for w in w_t]
    args = [x, shift, scale, *w_std, *w_t]
    scratch = []
    if forget is not None:
        wf, bf = forget
        sel, ones = _fox_bias_selection()
        lower = jnp.asarray(np.tril(np.ones((tm, tm), np.float32)), BF16)
        in_specs += [pl.BlockSpec(a.shape, const) for a in (wf, bf, sel, ones, lower)]
        args += [wf, bf, sel, ones, lower]
        out_shape += [jax.ShapeDtypeStruct((t, d), BF16), jax.ShapeDtypeStruct((FOX_HEADS, t), F32)]
        out_specs += [pl.BlockSpec((tm, d), lambda i: (i, 0)),
                      pl.BlockSpec((FOX_HEADS, tm), lambda i: (0, i))]
        scratch = [pltpu.VMEM((1, HEAD_BLOCK), F32)]
    return pl.pallas_call(
        functools.partial(_modproj_kernel, n_std=len(w_std), n_t=len(w_t),
                          with_forget=forget is not None, tiles_per_seq=tiles_per_seq),
        out_shape=tuple(out_shape),
        grid=(t // tm,),
        in_specs=in_specs,
        out_specs=tuple(out_specs),
        scratch_shapes=scratch,
        compiler_params=_params("arbitrary"),
        name="modproj_fox" if forget is not None else "modproj",
    )(*args)


def _bias_rows(entries, tq):
    rid = lax.broadcasted_iota(jnp.int32, (BIAS_ROWS, tq), 0)
    out = jnp.zeros((BIAS_ROWS, tq), F32)
    for r, v in entries.items():
        out = jnp.where(rid == r, v, out)
    return out


def _attn_kernel(*refs, mode, lam_init, n_heads):
    refs = list(refs)
    q_at, o_at = 0, (5 if mode == "fox" else 6)
    q_ref, o_ref = refs[q_at], refs[o_at]

    def query_tile(i, carry):
        rows = pl.ds(pl.multiple_of(i * TQ, TQ), TQ)
        tile_refs = list(refs)
        tile_refs[q_at] = q_ref.at[:, rows]
        tile_refs[o_at] = o_ref.at[rows, :]
        if mode == "fox":
            tile_refs[4] = refs[4].at[:, :, rows]
        _attn_query_tile(i, *tile_refs, mode=mode, lam_init=lam_init, n_heads=n_heads)
        return carry

    lax.fori_loop(0, q_ref.shape[1] // TQ, query_tile, 0)


def _attn_query_tile(i, *refs, mode, lam_init, n_heads):
    if mode == "fox":
        q_ref, k_ref, kb_ref, v_ref, cumt_ref, o_ref, m_ref, acc_ref = refs
    else:
        q_ref, k_ref, kb_ref, v_ref, lam_ref, g_ref, o_ref, m_ref, acc_ref = refs
    hp = pl.program_id(1)
    tq = q_ref.shape[1]

    qs = q_ref[...] * jnp.asarray(HALF ** -0.5 * LOG2E, BF16)
    zeros_half = jnp.zeros((HALF, tq), BF16)
    q_tops = (jnp.concatenate([qs[:HALF], zeros_half], axis=0),
              jnp.concatenate([zeros_half, qs[HALF:]], axis=0))
    pad = jnp.zeros((HEAD_BLOCK - BIAS_ROWS, tq), BF16)
    if mode == "fox":
        q_bias = []
        for a in range(2):
            c = _split3(cumt_ref[0, a:a + 1, :] * LOG2E)
            base = 6 * a
            rows = {base: 1.0, base + 1: 1.0, base + 2: 1.0,
                    base + 3: c[0], base + 4: c[1], base + 5: c[2]}
            q_bias.append(_bias_rows(rows, tq).astype(BF16))
    else:
        slope = jnp.exp2(jnp.full((1, 1), -8.0, F32) * (hp + 1).astype(F32) / n_heads)
        l3 = _split3(slope * LOG2E)
        qpos = (i * tq + lax.broadcasted_iota(jnp.int32, (1, tq), 1)).astype(F32)
        t3 = _split3(-(slope * LOG2E) * qpos)
        rows = {0: l3[0], 1: l3[0], 2: l3[1], 3: l3[1], 4: l3[2], 5: l3[2],
                6: t3[0], 7: t3[1], 8: t3[2]}
        q_bias = [_bias_rows(rows, tq).astype(BF16)] * 2
    q_aug = tuple(jnp.concatenate([q_tops[a], q_bias[a], pad], axis=0) for a in range(2))

    m_ref[...] = jnp.full_like(m_ref, NEG_INF)
    acc_ref[...] = jnp.zeros_like(acc_ref)
    ones_rows = jnp.ones((ONES_ROWS, TK), BF16)

    def key_slice(k0, nk):
        return pl.ds(pl.multiple_of(k0, nk), nk)

    def scores(a, k0, nk, q0, nq):
        ks = key_slice(k0, nk)
        k_aug = jnp.concatenate([k_ref[ks, :], kb_ref[ks, :]], axis=1)
        return jnp.dot(k_aug, q_aug[a][:, q0:q0 + nq], preferred_element_type=F32)

    v_rows = [slice(a * HALF, (a + 1) * HALF) if mode == "fox" else slice(0, HEAD_BLOCK)
              for a in range(2)]
    n_v = v_rows[0].stop - v_rows[0].start

    def accumulate(a, k0, nk, q0, nq, masked, s):
        ql = slice(q0, q0 + nq)
        v_aug = jnp.concatenate([v_ref[v_rows[a], key_slice(k0, nk)], ones_rows[:, :nk]], axis=0)
        if masked:
            keep = ((k0 + lax.broadcasted_iota(jnp.int32, s.shape, 0))
                    <= (i * tq + q0 + lax.broadcasted_iota(jnp.int32, s.shape, 1)))
            s = jnp.where(keep, s, NEG_INF)
        m_old = m_ref[a, :, ql]
        m_new = jnp.maximum(m_old, jnp.max(s, axis=0, keepdims=True))
        alpha = jnp.exp2(m_old - m_new)
        p = jnp.exp2(s - m_new).astype(BF16)
        acc_ref[a, :, ql] = alpha * acc_ref[a, :, ql] + jnp.dot(v_aug, p, preferred_element_type=F32)
        m_ref[a, :, ql] = m_new

    def full_block(j):
        return [(a, j * TK, TK, 0, tq, False) for a in range(2)]

    def diagonal_block(j):
        strip = TK // DIAG_STRIPS
        return [(a, j * TK + c * strip, strip, c * strip, tq - c * strip, True)
                for c in range(DIAG_STRIPS) for a in range(2)]

    def process(work):
        pending = [scores(*w[:5]) for w in work[:LOOKAHEAD]]
        for n, w in enumerate(work):
            s = pending.pop(0)
            if n + LOOKAHEAD < len(work):
                pending.append(scores(*work[n + LOOKAHEAD][:5]))
            accumulate(*w, s)

    n_full = (i * tq) // TK

    def loop_body(p, carry):
        process(sum((full_block(BLOCKS_PER_TRIP * p + d) for d in range(BLOCKS_PER_TRIP)), []))
        return carry

    lax.fori_loop(0, n_full // BLOCKS_PER_TRIP, loop_body, 0)

    for rem in range(BLOCKS_PER_TRIP):
        @pl.when(n_full % BLOCKS_PER_TRIP == rem)
        def _(rem=rem):
            process(sum((full_block(n_full - rem + d) for d in range(rem)), [])
                    + diagonal_block(n_full))

    inv = [1.0 / acc_ref[a, n_v:n_v + 1, :] for a in range(2)]
    if mode == "fox":
        o_t = jnp.concatenate([acc_ref[a, :n_v, :] * inv[a] for a in range(2)], axis=0)
        o_ref[...] = o_t.T.astype(o_ref.dtype)
    else:
        lp = lam_ref[0]
        lam = (jnp.exp(jnp.sum(lp[0:1] * lp[1:2], axis=-1, keepdims=True))
               - jnp.exp(jnp.sum(lp[2:3] * lp[3:4], axis=-1, keepdims=True)) + lam_init)
        o_t = acc_ref[0, :HEAD_BLOCK, :] * inv[0] - lam * (acc_ref[1, :HEAD_BLOCK, :] * inv[1])
        o = o_t.T
        o = o * lax.rsqrt(jnp.mean(o * o, axis=-1, keepdims=True) + LN_EPS)
        o_ref[...] = (o * g_ref[...] * (1.0 - lam_init)).astype(o_ref.dtype)


def _attention(mode, q_t, k_arr, k_off, kb, kb_spec, v_t, v_off, extras, extra_specs, batch, seq,
               lam_init=0.0, n_heads=DIFF_HEADS):
    assert TQ == TK, "one masked diagonal block per query tile"
    t = batch * seq
    acc_rows = (HALF if mode == "fox" else HEAD_BLOCK) + ONES_ROWS
    in_specs = [
        pl.BlockSpec((HEAD_BLOCK, seq), lambda b, h: (h, b)),
        pl.BlockSpec((seq, HEAD_BLOCK), lambda b, h: (b, k_off + h)),
        kb_spec,
        pl.BlockSpec((HEAD_BLOCK, seq), lambda b, h: (v_off + h, b)),
    ] + extra_specs
    return pl.pallas_call(
        functools.partial(_attn_kernel, mode=mode, lam_init=lam_init, n_heads=n_heads),
        out_shape=jax.ShapeDtypeStruct((t, D_MODEL), BF16),
        grid=(batch, N_HEAD_BLOCKS),
        in_specs=in_specs,
        out_specs=pl.BlockSpec((seq, HEAD_BLOCK), lambda b, h: (b, h)),
        scratch_shapes=[pltpu.VMEM((2, 1, TQ), F32),
                        pltpu.VMEM((2, acc_rows, TQ), F32)],
        compiler_params=_params("parallel", "parallel"),
        name="attn_" + mode,
    )(q_t, k_arr, kb, v_t, *extras)


def _bias_lanes(cols):
    lanes = jnp.stack(cols, axis=-1)
    pad = [(0, 0)] * (lanes.ndim - 1) + [(0, HEAD_BLOCK - lanes.shape[-1])]
    return jnp.pad(lanes, pad).astype(BF16)


def _alibi_key_bias(seq):
    kpos = jnp.arange(seq, dtype=jnp.int32)
    hi = ((kpos // 64) * 64).astype(F32)
    lo = (kpos % 64).astype(F32)
    one = jnp.ones((seq,), F32)
    return _bias_lanes([hi, lo, hi, lo, hi, lo, one, one, one])


SUBLANES = 8
LANES = 128
N_DMA_PRIORITIES = 2


def _store_token_tiled(ref, value):
    n = value.shape[0]
    for s in range(SUBLANES):
        ref[pl.ds(s, n, stride=SUBLANES), :] = value[:, s * LANES:(s + 1) * LANES]


def _load_token_tiled(ref, n):
    return jnp.concatenate([ref[pl.ds(s, n, stride=SUBLANES), :] for s in range(SUBLANES)], axis=1)


def _layer_norm(z, g, b):
    mu = jnp.mean(z, axis=-1, keepdims=True)
    zc = z - mu
    var = jnp.mean(zc * zc, axis=-1, keepdims=True)
    return zc * lax.rsqrt(var + LN_EPS) * g + b


def _argmax_first(vals):
    best, idx = vals[0], jnp.zeros(vals[0].shape, jnp.int32)
    for j in range(1, len(vals)):
        better = vals[j] > best
        idx = jnp.where(better, j, idx)
        best = jnp.where(better, vals[j], best)
    return idx


def _select(vals, idx):
    out = vals[0]
    for j in range(1, len(vals)):
        out = jnp.where(idx == j, vals[j], out)
    return out


def _post_attn_kernel(o_ref, x_ref, wo_ref, ga_ref, lng_ref, lnb_ref, shf_ref, scf_ref,
                      rwt_ref, rb_ref, upper_ref, xo_ref, h_ref, eidx_ref, wts_ref, rank_ref,
                      counts_ref, count_ref):
    y = jnp.dot(o_ref[...], wo_ref[...], preferred_element_type=F32)
    z = DEEPNORM_ALPHA * x_ref[...] + (1.0 + ga_ref[0]) * y
    xn = _layer_norm(z, lng_ref[...], lnb_ref[...])
    xo_ref[...] = xn
    h = xn * (1.0 + scf_ref[0]) + shf_ref[0]
    _store_token_tiled(h_ref, h)

    h_hi = h.astype(BF16)
    h_lo = (h - h_hi.astype(F32)).astype(BF16)
    logits = (lax.dot_general(rwt_ref[0], h_hi, NT_DIMS, preferred_element_type=F32)
              + lax.dot_general(rwt_ref[0], h_lo, NT_DIMS, preferred_element_type=F32)
              + lax.dot_general(rwt_ref[1], h_hi, NT_DIMS, preferred_element_type=F32))
    rows = [logits[e:e + 1, :] for e in range(N_EXPERTS)]
    mx = functools.reduce(jnp.maximum, rows)
    ex = [jnp.exp(r - mx) for r in rows]
    den = functools.reduce(lambda a, b: a + b, ex)
    probs = [e_ / den for e_ in ex]
    sel = [probs[e] + rb_ref[e:e + 1, :] for e in range(N_EXPERTS)]

    grp_scores = []
    for g in range(N_GROUPS):
        v = sel[g * EXPERTS_PER_GROUP:(g + 1) * EXPERTS_PER_GROUP]
        pair_sums = [v[a] + v[b] for a in range(EXPERTS_PER_GROUP) for b in range(a + 1, EXPERTS_PER_GROUP)]
        grp_scores.append(functools.reduce(jnp.maximum, pair_sums))
    g_idx = _argmax_first(grp_scores)

    in_sel = [_select([sel[g * EXPERTS_PER_GROUP + j] for g in range(N_GROUPS)], g_idx)
              for j in range(EXPERTS_PER_GROUP)]
    in_prob = [_select([probs[g * EXPERTS_PER_GROUP + j] for g in range(N_GROUPS)], g_idx)
               for j in range(EXPERTS_PER_GROUP)]
    i1 = _argmax_first(in_sel)
    b2 = jnp.full(in_sel[0].shape, -jnp.inf, F32)
    i2 = jnp.zeros(in_sel[0].shape, jnp.int32)
    for j in range(EXPERTS_PER_GROUP):
        cand = (i1 != j) & (in_sel[j] > b2)
        i2 = jnp.where(cand, j, i2)
        b2 = jnp.where(cand, in_sel[j], b2)
    p1 = _select(in_prob, i1)
    p2 = _select(in_prob, i2)
    tot = p1 + p2
    experts = (g_idx * EXPERTS_PER_GROUP + i1, g_idx * EXPERTS_PER_GROUP + i2)
    wts_ref[0:1, :] = p1 / tot
    wts_ref[1:2, :] = p2 / tot

    @pl.when(pl.program_id(0) == 0)
    def _():
        count_ref[...] = jnp.zeros_like(count_ref)

    tm = experts[0].shape[1]
    eid = lax.broadcasted_iota(jnp.int32, (N_EXPERTS, tm), 0)
    hit = [eid == e for e in experts]
    chosen = jnp.where(hit[0] | hit[1], 1.0, 0.0)
    before = jnp.dot(chosen.astype(BF16), upper_ref[...], preferred_element_type=F32) + count_ref[...]
    for k in range(TOP_K):
        eidx_ref[k:k + 1, :] = experts[k]
        rank_ref[k:k + 1, :] = jnp.sum(jnp.where(hit[k], before, 0.0), axis=0,
                                       keepdims=True).astype(jnp.int32)
    count_ref[...] += jnp.sum(chosen, axis=1, keepdims=True)
    counts_ref[...] = jnp.broadcast_to(count_ref[...], counts_ref.shape)


def _post_attn(o, x, wo, g_a, ln_g, ln_b, sh_f, sc_f, router_wt, router_b, seq):
    t, d = x.shape
    tm = TM_PROJ
    tiles_per_seq = seq // tm
    mod_spec = pl.BlockSpec((1, 1, d), lambda i: (i // tiles_per_seq, 0, 0))
    vec_spec = pl.BlockSpec((1, d), lambda i: (0, 0))
    tok_spec = pl.BlockSpec((tm, d), lambda i: (i, 0))
    route_spec = pl.BlockSpec((TOP_K, tm), lambda i: (0, i))
    upper = jnp.asarray(np.triu(np.ones((tm, tm), np.float32), k=1), BF16)
    return pl.pallas_call(
        _post_attn_kernel,
        out_shape=(jax.ShapeDtypeStruct((t, d), F32), jax.ShapeDtypeStruct((t * SUBLANES, LANES), F32),
                   jax.ShapeDtypeStruct((TOP_K, t), jnp.int32), jax.ShapeDtypeStruct((TOP_K, t), F32),
                   jax.ShapeDtypeStruct((TOP_K, t), jnp.int32),
                   jax.ShapeDtypeStruct((N_EXPERTS, LANES), F32)),
        grid=(t // tm,),
        in_specs=[
            tok_spec, tok_spec,
            pl.BlockSpec((d, d), lambda i: (0, 0)),
            mod_spec, vec_spec, vec_spec, mod_spec, mod_spec,
            pl.BlockSpec((2, N_EXPERTS, d), lambda i: (0, 0, 0)),
            pl.BlockSpec((N_EXPERTS, 1), lambda i: (0, 0)),
            pl.BlockSpec((tm, tm), lambda i: (0, 0)),
        ],
        out_specs=(tok_spec, pl.BlockSpec((tm * SUBLANES, LANES), lambda i: (i, 0)),
                   route_spec, route_spec, route_spec,
                   pl.BlockSpec((N_EXPERTS, LANES), lambda i: (0, 0))),
        scratch_shapes=[pltpu.VMEM((N_EXPERTS, 1), F32)],
        compiler_params=_params("arbitrary"),
        name="post_attn",
    )(o, x, wo, g_a, ln_g, ln_b, sh_f, sc_f, router_wt, router_b, upper)


def _routing_tables(eidx, rank, counts, rows_pad):
    t = eidx.shape[1]
    padded = ((counts + TM_EXPERT - 1) // TM_EXPERT) * TM_EXPERT
    ends = jnp.cumsum(padded)
    starts = ends - padded
    start_of = sum(jnp.where(eidx == e, starts[e], 0) for e in range(N_EXPERTS))
    pos = (start_of + rank).reshape(-1)
    n_tiles = rows_pad // TM_EXPERT
    tile_start = jnp.arange(n_tiles, dtype=jnp.int32) * TM_EXPERT
    tile_expert = jnp.minimum(jnp.sum((tile_start[:, None] >= ends[None, :]).astype(jnp.int32), axis=1),
                              N_EXPERTS - 1).astype(jnp.int32)
    n_valid = (ends[-1] // TM_EXPERT).astype(jnp.int32).reshape(1)
    pad_lo = (starts + counts).astype(jnp.int32)
    pad_hi = jnp.concatenate([ends[:-1], jnp.full((1,), rows_pad, ends.dtype)]).astype(jnp.int32)
    return pos.astype(jnp.int32), pad_lo, pad_hi, tile_expert, n_valid


def _row_tile(ref, row8):
    return ref.at[pl.ds(pl.multiple_of(row8, SUBLANES), SUBLANES)]


def _dispatch_kernel(pos8_ref, pad_lo_ref, pad_hi_ref, h_ref, hs_hbm, zero_ref, sem, pad_sem, *, n_tok):
    i = pl.program_id(0)
    tm = h_ref.shape[0] // SUBLANES
    base = i * tm

    def body(r, carry):
        for k in range(TOP_K):
            pltpu.make_async_copy(_row_tile(h_ref, r * SUBLANES),
                                  _row_tile(hs_hbm, pos8_ref[k * n_tok + base + r]),
                                  sem).start(priority=k % N_DMA_PRIORITIES)
        return carry

    lax.fori_loop(0, tm, body, 0, unroll=8)

    @pl.when(i == 0)
    def _():
        zero_ref[...] = jnp.zeros_like(zero_ref)
        for e in range(N_EXPERTS):
            lo, hi = pad_lo_ref[e], pad_hi_ref[e]

            def start(r, carry):
                pltpu.make_async_copy(zero_ref, _row_tile(hs_hbm, r * SUBLANES), pad_sem).start()
                return carry

            def wait(r, carry):
                pltpu.make_async_copy(zero_ref, _row_tile(hs_hbm, r * SUBLANES), pad_sem).wait()
                return carry

            lax.fori_loop(lo, hi, start, 0)
            lax.fori_loop(lo, hi, wait, 0)

    for k in range(TOP_K):
        pltpu.make_async_copy(h_ref, hs_hbm.at[pl.ds(0, tm * SUBLANES)], sem).wait()


def _dispatch(h_tiled, pos8, pad_lo, pad_hi, rows_pad):
    t = h_tiled.shape[0] // SUBLANES
    tm = TM_DISPATCH
    grid_spec = pltpu.PrefetchScalarGridSpec(
        num_scalar_prefetch=3,
        grid=(t // tm,),
        in_specs=[pl.BlockSpec((tm * SUBLANES, LANES), lambda i, *_: (i, 0))],
        out_specs=pl.BlockSpec(memory_space=pl.ANY),
        scratch_shapes=[pltpu.VMEM((SUBLANES, LANES), F32), pltpu.SemaphoreType.DMA(()),
                        pltpu.SemaphoreType.DMA(())],
    )
    return pl.pallas_call(
        functools.partial(_dispatch_kernel, n_tok=t),
        out_shape=jax.ShapeDtypeStruct((rows_pad * SUBLANES, LANES), F32),
        grid_spec=grid_spec,
        compiler_params=_params("arbitrary"),
        name="dispatch",
    )(pos8, pad_lo, pad_hi, h_tiled)


def _expert_kernel(te_ref, nv_ref, hs_ref, wg_ref, wu_ref, wd_ref, y_ref, wg_bf, wu_bf, wd_bf):
    i = pl.program_id(0)
    nv = nv_ref[0]

    @pl.when((i < nv) & ((i == 0) | (te_ref[i] != te_ref[jnp.maximum(i - 1, 0)])))
    def _():
        wg_bf[...] = wg_ref[0, 0].astype(BF16)
        wu_bf[...] = wu_ref[0, 0].astype(BF16)
        wd_bf[...] = wd_ref[0, 0].astype(BF16)

    @pl.when(i < nv)
    def _():
        hb = _load_token_tiled(hs_ref, TM_EXPERT).astype(BF16)
        g = jnp.dot(hb, wg_bf[...], preferred_element_type=F32)
        u = jnp.dot(hb, wu_bf[...], preferred_element_type=F32)
        a = g * (1.0 / (1.0 + jnp.exp(-g))) * u
        _store_token_tiled(y_ref, jnp.dot(a.astype(BF16), wd_bf[...], preferred_element_type=F32))

    @pl.when(i >= nv)
    def _():
        y_ref[...] = jnp.zeros_like(y_ref)


def _experts(hs, tile_expert, n_valid, wg, wu, wd, layer):
    d = wg.shape[2]
    rows_pad = hs.shape[0] // SUBLANES
    n_tiles = rows_pad // TM_EXPERT
    grid_spec = pltpu.PrefetchScalarGridSpec(
        num_scalar_prefetch=2,
        grid=(n_tiles,),
        in_specs=[
            pl.BlockSpec((TM_EXPERT * SUBLANES, LANES), lambda i, te, nv: (i, 0)),
            pl.BlockSpec((1, 1, d, D_EXPERT), lambda i, te, nv: (layer, te[i], 0, 0)),
            pl.BlockSpec((1, 1, d, D_EXPERT), lambda i, te, nv: (layer, te[i], 0, 0)),
            pl.BlockSpec((1, 1, D_EXPERT, d), lambda i, te, nv: (layer, te[i], 0, 0)),
        ],
        out_specs=pl.BlockSpec((TM_EXPERT * SUBLANES, LANES), lambda i, te, nv: (i, 0)),
        scratch_shapes=[pltpu.VMEM((d, D_EXPERT), BF16), pltpu.VMEM((d, D_EXPERT), BF16),
                        pltpu.VMEM((D_EXPERT, d), BF16)],
    )
    return pl.pallas_call(
        _expert_kernel,
        out_shape=jax.ShapeDtypeStruct((rows_pad * SUBLANES, LANES), F32),
        grid_spec=grid_spec,
        compiler_params=_params("arbitrary"),
        name="experts",
    )(tile_expert, n_valid, hs, wg, wu, wd)


def _combine_kernel(pos_ref, y_hbm, x_ref, w_ref, gf_ref, lng_ref, lnb_ref, o_ref, buf, sem, *, n_tok):
    i = pl.program_id(0)
    n = pl.num_programs(0)
    slot = i % 2
    tm = x_ref.shape[0]

    def issue(tile, slot_):
        base = tile * tm

        def body(r, carry):
            for k in range(TOP_K):
                pltpu.make_async_copy(_row_tile(y_hbm, pos_ref[k * n_tok + base + r]),
                                      _row_tile(buf.at[slot_, k], r * SUBLANES),
                                      sem.at[slot_]).start(priority=k % N_DMA_PRIORITIES)
            return carry

        lax.fori_loop(0, tm, body, 0, unroll=8)

    @pl.when(i == 0)
    def _():
        issue(0, 0)

    @pl.when(i + 1 < n)
    def _():
        issue(i + 1, 1 - slot)

    for k in range(TOP_K):
        pltpu.make_async_copy(y_hbm.at[pl.ds(0, tm * SUBLANES)], buf.at[slot, k], sem.at[slot]).wait()
    w = w_ref[...]
    m = (w[:, 0:1] * _load_token_tiled(buf.at[slot, 0], tm)
         + w[:, 1:2] * _load_token_tiled(buf.at[slot, 1], tm))
    z = DEEPNORM_ALPHA * x_ref[...] + (1.0 + gf_ref[0]) * m
    o_ref[...] = _layer_norm(z, lng_ref[...], lnb_ref[...])


def _combine(pos, y, x, wts_t, g_f, ln_g, ln_b, seq):
    t, d = x.shape
    tm = TM_COMBINE
    tiles_per_seq = seq // tm
    grid_spec = pltpu.PrefetchScalarGridSpec(
        num_scalar_prefetch=1,
        grid=(t // tm,),
        in_specs=[
            pl.BlockSpec(memory_space=pl.ANY),
            pl.BlockSpec((tm, d), lambda i, pos_: (i, 0)),
            pl.BlockSpec((tm, TOP_K), lambda i, pos_: (i, 0)),
            pl.BlockSpec((1, 1, d), lambda i, pos_: (i // tiles_per_seq, 0, 0)),
            pl.BlockSpec((1, d), lambda i, pos_: (0, 0)),
            pl.BlockSpec((1, d), lambda i, pos_: (0, 0)),
        ],
        out_specs=pl.BlockSpec((tm, d), lambda i, pos_: (i, 0)),
        scratch_shapes=[pltpu.VMEM((2, TOP_K, tm * SUBLANES, LANES), F32),
                        pltpu.SemaphoreType.DMA((2,))],
    )
    return pl.pallas_call(
        functools.partial(_combine_kernel, n_tok=t),
        out_shape=jax.ShapeDtypeStruct((t, d), F32),
        grid_spec=grid_spec,
        compiler_params=_params("arbitrary"),
        name="combine",
    )(pos, y, x, wts_t, g_f, ln_g, ln_b)


def kernel(x, c, ada_w, ada_b, ln_attn_g, ln_attn_b, ln_ffn_g, ln_ffn_b, fox_w_in, fox_b_f, fox_w_o,
           kv_ada_w, kv_ada_b, kv_w, diff_w_q, diff_lambda, diff_norm_g, diff_w_o, router_w, router_b,
           moe_w_gate, moe_w_up, moe_w_down):
    bsz, seq, d = x.shape
    t = bsz * seq
    rows_pad = TOP_K * t + N_EXPERTS * TM_EXPERT

    mod = _ada(c, ada_w, ada_b, 1536)
    kv_mod = _ada(c, kv_ada_w[None], kv_ada_b[None], 1024)[0]

    def mod_part(l, k):
        return mod[l, :, k * d:(k + 1) * d].reshape(bsz, 1, d)

    rw_hi = router_w.T.astype(BF16)
    rw_lo = (router_w.T - rw_hi.astype(F32)).astype(BF16)
    router_wt = jnp.stack([rw_hi, rw_lo])
    router_bc = router_b.reshape(N_EXPERTS, 1)
    xt = x.reshape(t, d)

    k_sh = v_sh_t = None
    alibi_kb = _alibi_key_bias(seq)
    for l in range(DEPTH):
        sh_a, sc_a, g_a, sh_f, sc_f, g_f = (mod_part(l, k) for k in range(6))
        if l < N_A:
            w_in = fox_w_in[l]
            w_qt = w_in[:, :d].T.astype(BF16)
            w_k = w_in[:, d:2 * d].astype(BF16)
            w_vt = w_in[:, 2 * d:3 * d].T.astype(BF16)
            w_f = jnp.pad(w_in[:, 3 * d:], ((0, 0), (0, HEAD_BLOCK - FOX_HEADS))).astype(BF16)
            b_f = jnp.pad(fox_b_f[l], (0, HEAD_BLOCK - FOX_HEADS)).reshape(1, HEAD_BLOCK)
            k_arr, q_t, v_t, key_bias, cum_t = _modproj(xt, sh_a, sc_a, [w_k], [w_qt, w_vt], seq,
                                                        forget=(w_f, b_f))
            o = _attention(
                "fox", q_t, k_arr, 0, key_bias,
                pl.BlockSpec((seq, HEAD_BLOCK), lambda b, h: (b, h)),
                v_t, 0, (cum_t.reshape(N_HEAD_BLOCKS, 2, t),),
                [pl.BlockSpec((1, 2, seq), lambda b, h: (h, 0, b))],
                bsz, seq)
            w_o = fox_w_o[l].astype(BF16)
        else:
            j = l - N_A
            if l == N_A:
                k_sh, v_sh_t = _modproj(xt, kv_mod[:, :d].reshape(bsz, 1, d),
                                        kv_mod[:, d:].reshape(bsz, 1, d),
                                        [kv_w[:, :d].astype(BF16)], [kv_w[:, d:].T.astype(BF16)], seq)
            (q_t,) = _modproj(xt, sh_a, sc_a, [], [diff_w_q[j].T.astype(BF16)], seq)
            lam_init = 0.8 - 0.6 * math.exp(-0.3 * l)
            o = _attention(
                "diff", q_t, k_sh, 0, alibi_kb,
                pl.BlockSpec((seq, HEAD_BLOCK), lambda b, h: (0, 0)),
                v_sh_t, 0,
                (diff_lambda[j][None], diff_norm_g[j].reshape(1, HEAD_BLOCK)),
                [pl.BlockSpec((1, 4, HALF), lambda b, h: (0, 0, 0)),
                 pl.BlockSpec((1, HEAD_BLOCK), lambda b, h: (0, 0))],
                bsz, seq, lam_init=lam_init)
            w_o = diff_w_o[j].astype(BF16)

        xt, h, eidx, wts, rank, counts = _post_attn(
            o, xt, w_o, g_a, ln_attn_g[l].reshape(1, d), ln_attn_b[l].reshape(1, d), sh_f, sc_f,
            router_wt, router_bc, seq)
        pos, pad_lo, pad_hi, tile_expert, n_valid = _routing_tables(
            eidx, rank, counts[:, 0].astype(jnp.int32), rows_pad)
        pos8 = pos * SUBLANES
        hs = _dispatch(h, pos8, pad_lo, pad_hi, rows_pad)
        y = _experts(hs, tile_expert, n_valid, moe_w_gate, moe_w_up, moe_w_down, l)
        xt = _combine(pos8, y, xt, wts.T, g_f, ln_ffn_g[l].reshape(1, d),
                      ln_ffn_b[l].reshape(1, d), seq)
    return xt.reshape(bsz, seq, d)
```
